```python
import math
import jax, jax.numpy as jnp
from jax import lax
import numpy as np

D_MODEL = 1024
BATCH = 16
SEQ = 2048
DEPTH = 4

N_MIXERS = 2
N_HGRN_LAYERS = (DEPTH + 1) // 2
N_DSA_LAYERS = DEPTH // 2

HGRN_EXPAND = 128
HGRN_HEADS = D_MODEL // HGRN_EXPAND
HGRN_DK = HGRN_EXPAND
HGRN_DV = D_MODEL // HGRN_HEADS
HGRN_CHUNK = 32

ATT_HEADS = 16
ATT_HEAD_DIM = D_MODEL // ATT_HEADS
ATT_KV_HEADS = 2
ATT_GROUP = ATT_HEADS // ATT_KV_HEADS
IDX_HEADS = 8
IDX_HEAD_DIM = 64
TOPK_MAX = 256
Q_BLOCK = 128
DSA_SPLITS = (ATT_HEADS * ATT_HEAD_DIM, ATT_KV_HEADS * ATT_HEAD_DIM, ATT_KV_HEADS * ATT_HEAD_DIM,
              IDX_HEADS * IDX_HEAD_DIM, IDX_HEAD_DIM, IDX_HEADS)
DSA_IN_WIDTH = sum(DSA_SPLITS)

REL_BUCKETS = 32
REL_MAX_DIST = 128

D_FF = 2816
CONV_WIDTH = 3

EPS = 1e-6
NEG_BIG = -1e30
TINY = 1e-30

kernel_name = "hybrid_hgrn2_dsa_convffn_trunk"


def rms_norm(x, gain):
    xf = x.astype(jnp.float32)
    y = xf * lax.rsqrt(jnp.mean(xf * xf, axis=-1, keepdims=True) + EPS)
    return (y * gain.astype(jnp.float32)).astype(x.dtype)


def hgrn2_mixer(x, w_in, w_out, gate_norm, lower_bound):
    B, L, _ = x.shape
    n_chunks = L // HGRN_CHUNK
    proj = x @ w_in
    q, f, i, g = jnp.split(proj, 4, axis=-1)
    f32 = f.astype(jnp.float32)
    lb = lower_bound.astype(jnp.float32)
    forget = lb + (1.0 - lb) * jax.nn.sigmoid(f32)
    log_f = jnp.log(jnp.maximum(forget, TINY))
    k = (1.0 - lb) * jax.nn.sigmoid(-f32)

    def to_chunks(t, d):
        t = t.astype(jnp.float32).reshape(B, n_chunks, HGRN_CHUNK, HGRN_HEADS, d)
        return t.transpose(1, 0, 3, 2, 4)

    qc, kc, ic, gc = to_chunks(q, HGRN_DK), to_chunks(k, HGRN_DK), to_chunks(i, HGRN_DV), to_chunks(log_f, HGRN_DK)
    causal = jnp.tril(jnp.ones((HGRN_CHUNK, HGRN_CHUNK), dtype=bool))[None, None, :, :, None]

    def step(S, inp):
        qb, kb, ib, gb = inp
        b = jnp.cumsum(gb, axis=2)
        diff = b[:, :, :, None, :] - b[:, :, None, :, :]
        decay = jnp.where(causal, jnp.exp(jnp.where(causal, diff, 0.0)), 0.0)
        scores = jnp.einsum('bhtd,bhsd,bhtsd->bhts', qb, kb, decay)
        o = jnp.einsum('bhts,bhsv->bhtv', scores, ib) + jnp.einsum('bhtd,bhdv->bhtv', qb * jnp.exp(b), S)
        b_last = b[:, :, -1:, :]
        S_new = jnp.exp(b_last[:, :, 0, :])[..., None] * S + jnp.einsum('bhsd,bhsv->bhdv', kb * jnp.exp(b_last - b), ib)
        return S_new, o

    S0 = jnp.zeros((B, HGRN_HEADS, HGRN_DK, HGRN_DV), jnp.float32)
    _, o = lax.scan(step, S0, (qc, kc, ic, gc))
    o = o.transpose(1, 0, 3, 2, 4).reshape(B, L, HGRN_HEADS, HGRN_DV)
    o = o * lax.rsqrt(jnp.mean(o * o, axis=-1, keepdims=True) + EPS) * gate_norm.astype(jnp.float32)
    o = o * jax.nn.silu(g.astype(jnp.float32)).reshape(B, L, HGRN_HEADS, HGRN_DV)
    return o.reshape(B, L, D_MODEL).astype(x.dtype) @ w_out


def t5_causal_bucket(rel):
    n = jnp.maximum(rel, 0)
    max_exact = REL_BUCKETS // 2
    nf = jnp.maximum(n, max_exact).astype(jnp.float32)
    large = max_exact + (jnp.log(nf / max_exact) / math.log(REL_MAX_DIST / max_exact)
                         * (REL_BUCKETS - max_exact)).astype(jnp.int32)
    large = jnp.minimum(large, REL_BUCKETS - 1)
    return jnp.where(n < max_exact, n, large)


def dsa_mixer(x, w_in, w_out, rel_bias):
    B, L, _ = x.shape
    top_k = min(TOPK_MAX, L // 4)
    n_blocks = L // Q_BLOCK
    proj = x @ w_in
    cuts = np.cumsum(DSA_SPLITS)[:-1].tolist()
    q, k, v, q_idx, k_idx, w_idx = jnp.split(proj, cuts, axis=-1)
    q = q.reshape(B, L, ATT_KV_HEADS, ATT_GROUP, ATT_HEAD_DIM)
    q_idx = q_idx.reshape(B, L, IDX_HEADS, IDX_HEAD_DIM)
    w_idx = w_idx * (IDX_HEADS ** -0.5 * IDX_HEAD_DIM ** -0.5)

    def blockify(t):
        return jnp.moveaxis(t.reshape(B, n_blocks, Q_BLOCK, *t.shape[2:]), 1, 0)

    key_pos = jnp.arange(L, dtype=jnp.int32)
    starts = jnp.arange(n_blocks, dtype=jnp.int32) * Q_BLOCK

    def one_block(args):
        qb, qib, wb, start = args
        q_pos = start + jnp.arange(Q_BLOCK, dtype=jnp.int32)
        dots = jnp.einsum('bthd,bsd->bths', qib, k_idx).astype(jnp.float32)
        scores = jnp.einsum('bths,bth->bts', jax.nn.relu(dots), wb.astype(jnp.float32))
        admissible = key_pos[None, :] <= q_pos[:, None]
        scores = jnp.where(admissible[None], scores, NEG_BIG)
        _, sel = lax.top_k(scores, top_k)
        k_sel = jax.vmap(lambda kk, ii: kk[ii])(k, sel).reshape(B, Q_BLOCK, top_k, ATT_KV_HEADS, ATT_HEAD_DIM)
        v_sel = jax.vmap(lambda vv, ii: vv[ii])(v, sel).reshape(B, Q_BLOCK, top_k, ATT_KV_HEADS, ATT_HEAD_DIM)
        logits = jnp.einsum('btkgd,btskd->btkgs', qb, k_sel).astype(jnp.float32) * (ATT_HEAD_DIM ** -0.5)
        rel = q_pos[None, :, None] - sel
        bias = rel_bias.astype(jnp.float32)[t5_causal_bucket(rel)]
        bias = bias.reshape(B, Q_BLOCK, top_k, ATT_KV_HEADS, ATT_GROUP).transpose(0, 1, 3, 4, 2)
        valid = (rel >= 0)[:, :, None, None, :]
        logits = jnp.where(valid, logits + bias, NEG_BIG)
        probs = jax.nn.softmax(logits, axis=-1)
        out = jnp.einsum('btkgs,btskd->btkgd', probs, v_sel.astype(jnp.float32))
        return out.reshape(B, Q_BLOCK, D_MODEL).astype(x.dtype)

    out = lax.map(one_block, (blockify(q), blockify(q_idx), blockify(w_idx), starts))
    out = jnp.moveaxis(out, 0, 1).reshape(B, L, D_MODEL)
    return out @ w_out


def conv_ffn(x, w_up, conv_w, conv_b, w_down):
    h = x @ w_up
    C = h.shape[-1]
    h = lax.conv_general_dilated(h, conv_w[:, None, :].astype(h.dtype), window_strides=(1,),
                                 padding=[(CONV_WIDTH - 1, 0)],
                                 dimension_numbers=('NWC', 'WIO', 'NWC'),
                                 feature_group_count=C) + conv_b
    gate, up = jnp.split(h, 2, axis=-1)
    return (jax.nn.silu(gate) * up) @ w_down


def setup_inputs(seed: int = 0) -> dict:
    key = jax.random.key(seed)
    ks = jax.random.split(key, 16)
    D = D_MODEL

    def dense(k, shape, fan_in):
        return jax.random.normal(k, shape, jnp.float32) * fan_in ** -0.5

    return {
        "x": jax.random.normal(ks[0], (BATCH, SEQ, D), jnp.float32),
        "attn_norm": 1.0 + 0.05 * jax.random.normal(ks[1], (DEPTH, D), jnp.float32),
        "ffn_norm": 1.0 + 0.05 * jax.random.normal(ks[2], (DEPTH, D), jnp.float32),
        "hgrn_w_in": dense(ks[3], (N_HGRN_LAYERS, D, 4 * D), D),
        "hgrn_w_out": dense(ks[4], (N_HGRN_LAYERS, D, D), D),
        "hgrn_gate_norm": 1.0 + 0.05 * jax.random.normal(ks[5], (N_HGRN_LAYERS, HGRN_DV), jnp.float32),
        "hgrn_lower_bounds": 0.1 * jax.random.normal(ks[6], (N_HGRN_LAYERS, D), jnp.float32),
        "dsa_w_in": dense(ks[7], (N_DSA_LAYERS, D, DSA_IN_WIDTH), D),
        "dsa_w_out": dense(ks[8], (N_DSA_LAYERS, D, D), D),
        "rel_bias": 0.1 * jax.random.normal(ks[9], (REL_BUCKETS, ATT_HEADS), jnp.float32),
        "ffn_w_up": dense(ks[10], (DEPTH, D, 2 * D_FF), D),
        "ffn_conv_w": dense(ks[11], (DEPTH, CONV_WIDTH, 2 * D_FF), CONV_WIDTH),
        "ffn_conv_b": 0.01 * jax.random.normal(ks[12], (DEPTH, 2 * D_FF), jnp.float32),
        "ffn_w_down": dense(ks[13], (DEPTH, D_FF, D), D_FF),
        "final_norm": 1.0 + 0.05 * jax.random.normal(ks[14], (D,), jnp.float32),
    }


def reference(x, attn_norm, ffn_norm, hgrn_w_in, hgrn_w_out, hgrn_gate_norm, hgrn_lower_bounds,
              dsa_w_in, dsa_w_out, rel_bias, ffn_w_up, ffn_conv_w, ffn_conv_b, ffn_w_down, final_norm):
    lb_soft = jax.nn.softmax(hgrn_lower_bounds.astype(jnp.float32), axis=0)
    lower_bounds = jnp.cumsum(lb_soft, axis=0) - lb_soft[0]
    h = x
    for layer in range(DEPTH):
        j = layer // N_MIXERS
        hn = rms_norm(h, attn_norm[layer])
        if layer % N_MIXERS == 0:
            mix = hgrn2_mixer(hn, hgrn_w_in[j], hgrn_w_out[j], hgrn_gate_norm[j], lower_bounds[j])
        else:
            mix = dsa_mixer(hn, dsa_w_in[j], dsa_w_out[j], rel_bias)
        h = h + mix
        h = h + conv_ffn(rms_norm(h, ffn_norm[layer]), ffn_w_up[layer], ffn_conv_w[layer],
                         ffn_conv_b[layer], ffn_w_down[layer])
    return rms_norm(h, final_norm)
```

```python
import functools
import math

import jax
import jax.numpy as jnp
from jax import lax
from jax.experimental import pallas as pl
from jax.experimental.pallas import tpu as pltpu

F32 = jnp.float32
BF16 = jnp.bfloat16
I32 = jnp.int32

EPS = 1e-6
NEG_BIG = -1e30
TINY = 1e-30

HGRN_HEAD_DIM = 128
HGRN_BLOCK = 128
ATT_HEADS = 16
ATT_HEAD_DIM = 64
ATT_KV_HEADS = 2
ATT_GROUP = ATT_HEADS // ATT_KV_HEADS
IDX_HEADS = 8
IDX_HEAD_DIM = 64
TOPK_MAX = 256
REL_BUCKETS = 32
REL_MAX_DIST = 128
KEY_BLOCK = 128
Q_BLOCK = 128
INT_MIN = -(2 ** 31)

VMEM_LIMIT_BYTES = 56 * 1024 * 1024


def _dot(a, b):
    return jnp.dot(a, b, preferred_element_type=F32)


def _dot_nt(a, b):
    return lax.dot_general(a, b, (((1,), (1,)), ((), ())), preferred_element_type=F32)


def _rms_norm(x, gain):
    y = x * lax.rsqrt(jnp.mean(x * x, axis=-1, keepdims=True) + EPS)
    return y * gain


def _params(n_axes):
    return pltpu.CompilerParams(dimension_semantics=("arbitrary",) * n_axes,
                                vmem_limit_bytes=VMEM_LIMIT_BYTES)


def _block_rows(a, m):
    c = a.shape[0]
    return a.reshape(c // m, m, a.shape[1])[:, m - 1:m, :]


def _spread_rows(r, m):
    n = r.shape[0]
    return jnp.broadcast_to(r, (n, m, r.shape[2])).reshape(n * m, r.shape[2])


def _hgrn_kernel(x_ref, gain_ref, w_ref, lbraw_ref, gn_ref, o_ref, xn_scr, proj_scr, st_scr,
                 *, layer_j, seq):
    C = HGRN_BLOCK
    dh = HGRN_HEAD_DIM
    head = pl.program_id(1)

    @pl.when(head == 0)
    def _():
        rb = 256

        def norm_rows(r, carry):
            rows = pl.ds(pl.multiple_of(r * rb, rb), rb)
            xn_scr[rows, :] = _rms_norm(x_ref[0, rows, :], gain_ref[...]).astype(BF16)
            return carry
        lax.fori_loop(0, seq // rb, norm_rows, 0)

    pb = 512

    def proj_rows(r, carry):
        rows = pl.ds(pl.multiple_of(r * pb, pb), pb)
        proj_scr[rows, :] = _dot(xn_scr[rows, :], w_ref[0])
        return carry
    lax.fori_loop(0, seq // pb, proj_rows, 0)

    raw = lbraw_ref[...]
    ex = jnp.exp(raw - jnp.max(raw, axis=0, keepdims=True))
    soft = ex / jnp.sum(ex, axis=0, keepdims=True)
    lb = jnp.zeros((1, dh), F32)
    for l in range(1, layer_j + 1):
        lb = lb + soft[l:l + 1, :]
    oml = 1.0 - lb
    gn = gn_ref[...]

    ri = lax.broadcasted_iota(I32, (C, C), 0)
    ci = lax.broadcasted_iota(I32, (C, C), 1)
    tril = (ci <= ri).astype(BF16)
    trow = lax.broadcasted_iota(I32, (C, dh), 0)
    sub = trow & 7

    def same_block(n):
        sh = int(math.log2(n))
        return ((ri >> sh) == (ci >> sh)).astype(F32)

    st_scr[...] = jnp.zeros_like(st_scr)

    def step(c, carry):
        rows = pl.ds(pl.multiple_of(c * C, C), C)
        q = proj_scr[rows, 0:dh]
        f = proj_scr[rows, dh:2 * dh]
        iv = proj_scr[rows, 2 * dh:3 * dh]
        g = proj_scr[rows, 3 * dh:4 * dh]

        e = jnp.exp(-jnp.abs(f))
        r = 1.0 / (1.0 + e)
        er = e * r
        pos = f >= 0.0
        forget = lb + oml * jnp.where(pos, r, er)
        logf = jnp.log(jnp.maximum(forget, TINY))
        k = oml * jnp.where(pos, er, r)

        hi = logf.astype(BF16)
        r1 = logf - hi.astype(F32)
        mid = r1.astype(BF16)
        lo = (r1 - mid.astype(F32)).astype(BF16)
        b = _dot(tril, hi) + _dot(tril, mid) + _dot(tril, lo)

        b8 = b.reshape(C // 8, 8, dh)
        qs, ks = [], []
        for rho in range(8):
            bref = _spread_rows(b8[:, rho:rho + 1, :], 8)
            qs.append(jnp.where(sub >= rho, q * jnp.exp(b - bref), 0.0).astype(BF16))
            ks.append(jnp.where(sub == rho, k, 0.0).astype(BF16))
        acc = _dot_nt(jnp.concatenate(qs, axis=1), jnp.concatenate(ks, axis=1))
        for m in (8, 16, 32, 64):
            last = _block_rows(b, m)
            prev = jnp.concatenate([jnp.zeros((1, 1, dh), F32), last[:-1]], axis=0)
            upper = ((trow >> int(math.log2(m))) & 1) == 1
            qm = jnp.where(upper, q * jnp.exp(b - _spread_rows(prev, m)), 0.0).astype(BF16)
            km = jnp.where(upper, 0.0, k * jnp.exp(_spread_rows(last, m) - b)).astype(BF16)
            acc = _dot_nt(qm, km) + same_block(m) * acc

        st = st_scr[...]
        o = _dot(acc.astype(BF16), iv.astype(BF16)) + _dot_nt((q * jnp.exp(b)).astype(BF16), st.astype(BF16))
        b_last = b[C - 1:C, :]
        kh = (k * jnp.exp(b_last - b)).astype(BF16)
        st_scr[...] = st * jnp.exp(b_last) + _dot(iv.T.astype(BF16), kh)

        o = o * lax.rsqrt(jnp.mean(o * o, axis=-1, keepdims=True) + EPS) * gn
        o = o * (g * jax.nn.sigmoid(g))
        o_ref[0, rows, :] = o.astype(o_ref.dtype)
        return carry

    lax.fori_loop(0, seq // C, step, 0)


def _hgrn_mixer(h3, gain, w_in, lower_raw, gate_norm, layer_j):
    B, L, D = h3.shape
    dh = HGRN_HEAD_DIM
    H = D // dh
    n_layers = lower_raw.shape[0]
    w = w_in.reshape(D, 4, H, dh).transpose(2, 0, 1, 3).reshape(H, D, 4 * dh).astype(BF16)
    return pl.pallas_call(
        functools.partial(_hgrn_kernel, layer_j=layer_j, seq=L),
        grid=(B, H),
        in_specs=[
            pl.BlockSpec((1, L, D), lambda b, h: (b, 0, 0)),
            pl.BlockSpec((1, D), lambda b, h: (0, 0)),
            pl.BlockSpec((1, D, 4 * dh), lambda b, h: (h, 0, 0)),
            pl.BlockSpec((n_layers, dh), lambda b, h: (0, h)),
            pl.BlockSpec((1, dh), lambda b, h: (0, 0)),
        ],
        out_specs=pl.BlockSpec((1, L, dh), lambda b, h: (b, 0, h)),
        out_shape=jax.ShapeDtypeStruct((B, L, D), BF16),
        scratch_shapes=[pltpu.VMEM((L, D), BF16), pltpu.VMEM((L, 4 * dh), F32), pltpu.VMEM((dh, dh), F32)],
        compiler_params=_params(2),
        name="hgrn_mixer",
    )(h3, gain.reshape(1, D), w, lower_raw, gate_norm.reshape(1, dh))


def _dsa_proj_kernel(x_ref, gain_ref, w_ref, q_ref, k_ref, vt_ref, qi_ref, ki_ref, wt_ref):
    nq = ATT_HEADS * ATT_HEAD_DIM
    nkv = ATT_KV_HEADS * ATT_HEAD_DIM
    nqi = IDX_HEADS * IDX_HEAD_DIM
    xn = _rms_norm(x_ref[0], gain_ref[...]).astype(BF16)
    p = _dot(xn, w_ref[...])
    c0 = 0
    q_ref[0] = (p[:, c0:c0 + nq] * (ATT_HEAD_DIM ** -0.5)).astype(BF16)
    c0 += nq
    k_ref[0] = p[:, c0:c0 + nkv].astype(BF16)
    c0 += nkv
    vt_ref[0] = p[:, c0:c0 + nkv].T.astype(BF16)
    c0 += nkv
    qi_ref[0] = p[:, c0:c0 + nqi].astype(BF16)
    c0 += nqi
    ki_ref[0] = p[:, c0:c0 + IDX_HEAD_DIM].astype(BF16)
    tail = p[:, c0:c0 + 128].T
    wt_ref[0] = tail[IDX_HEAD_DIM:IDX_HEAD_DIM + IDX_HEADS, :] * (IDX_HEADS ** -0.5 * IDX_HEAD_DIM ** -0.5)


def _bias_kernel(rb_ref, out_ref):
    kb = KEY_BLOCK
    s = lax.broadcasted_iota(I32, (kb, kb), 0)
    t = lax.broadcasted_iota(I32, (kb, kb), 1)
    max_exact = REL_BUCKETS // 2
    for tile in range(2):
        n = jnp.maximum(t - s + kb * tile, 0)
        nf = jnp.maximum(n, max_exact).astype(F32)
        large = max_exact + (jnp.log(nf / max_exact) / math.log(REL_MAX_DIST / max_exact)
                             * (REL_BUCKETS - max_exact)).astype(I32)
        large = jnp.minimum(large, REL_BUCKETS - 1)
        bucket = jnp.where(n < max_exact, n, large)
        for h in range(ATT_HEADS):
            def pick(bk, acc):
                return jnp.where(bucket == bk, rb_ref[bk, h], acc)
            tile_h = lax.fori_loop(0, REL_BUCKETS, pick, jnp.zeros((kb, kb), F32))
            out_ref[h // ATT_GROUP, tile, :, (h % ATT_GROUP) * kb:(h % ATT_GROUP + 1) * kb] = (
                tile_h - rb_ref[REL_BUCKETS - 1, h])


def _dsa_attn_kernel(q_ref, k_ref, vt_ref, qi_ref, ki_ref, wt_ref, bias_ref, o_ref,
                     qs_scr, qis_scr, key_scr, mask_scr, m_scr, l_scr, acc_scr, out_scr, thr_scr,
                     *, top_k, seq):
    tq = Q_BLOCK
    kb = KEY_BLOCK
    hd = ATT_HEAD_DIM
    i = pl.program_id(1)
    n_blocks = i + 1

    for h in range(ATT_HEADS):
        qs_scr[h * tq:(h + 1) * tq, :] = q_ref[0, :, h * hd:(h + 1) * hd]
    for h in range(IDX_HEADS):
        qis_scr[h * tq:(h + 1) * tq, :] = qi_ref[0, :, h * IDX_HEAD_DIM:(h + 1) * IDX_HEAD_DIM]
    wt = wt_ref[0]

    s_loc = lax.broadcasted_iota(I32, (kb, tq), 0)
    t_loc = lax.broadcasted_iota(I32, (kb, tq), 1)
    future = s_loc > t_loc

    def idx_block(c, carry):
        rows = pl.ds(pl.multiple_of(c * kb, kb), kb)
        d = _dot_nt(ki_ref[0, rows, :], qis_scr[...])
        sc = jnp.zeros((kb, tq), F32)
        for h in range(IDX_HEADS):
            sc = sc + wt[h:h + 1, :] * jnp.maximum(d[:, h * tq:(h + 1) * tq], 0.0)
        sc = jnp.where((c == i) & future, NEG_BIG, sc) + 0.0
        bits = pltpu.bitcast(sc, I32)
        key_scr[rows, :] = jnp.where(bits < 0, bits ^ 0x7FFFFFFF, bits)
        return carry
    lax.fori_loop(0, n_blocks, idx_block, 0)
    pad_rows = pl.ds(pl.multiple_of(n_blocks * kb, kb), kb)
    key_scr[pad_rows, :] = jnp.full((kb, tq), INT_MIN, I32)

    sb = 2 * kb
    n_steps = (n_blocks + 1) // 2

    def count(pred):
        def body(r, cnt):
            rows = pl.ds(pl.multiple_of(r * sb, sb), sb)
            s_glob = r * sb + lax.broadcasted_iota(I32, (sb, tq), 0)
            return cnt + jnp.sum(jnp.where(pred(key_scr[rows, :], s_glob), 1, 0), axis=0, keepdims=True)
        return lax.fori_loop(0, n_steps, body, jnp.zeros((1, tq), I32))

    thr_scr[0:1, :] = jnp.full((1, tq), INT_MIN, I32)
    thr_scr[1:2, :] = jnp.full((1, tq), 2 ** 31 - 1, I32)

    @pl.when(n_blocks * kb > top_k)
    def _():
        c0 = count(lambda key, s: key >= 0)
        t0 = jnp.where(c0 >= top_k, 0, INT_MIN).astype(I32)

        def bit_step(it, t_cur):
            cand = t_cur | jnp.left_shift(jnp.int32(1), 30 - it)
            cnt = count(lambda key, s: key >= cand)
            return jnp.where(cnt >= top_k, cand, t_cur)
        thr = lax.fori_loop(0, 31, bit_step, t0)
        thr_scr[0:1, :] = thr

        n_ge = count(lambda key, s: key >= thr)

        @pl.when(jnp.max(n_ge) > top_k)
        def _():
            need = top_k - count(lambda key, s: key > thr)

            pos_bits = (seq - 1).bit_length()

            def pos_step(it, j_cur):
                cand = j_cur | jnp.left_shift(jnp.int32(1), pos_bits - 1 - it)
                cnt = count(lambda key, s: (key == thr) & (s < cand))
                return jnp.where(cnt < need, cand, j_cur)
            thr_scr[1:2, :] = lax.fori_loop(0, pos_bits, pos_step, jnp.zeros((1, tq), I32))

    thr = thr_scr[0:1, :]
    j_last = thr_scr[1:2, :]

    def mask_block(c, carry):
        rows = pl.ds(pl.multiple_of(c * kb, kb), kb)
        key = key_scr[rows, :]
        s_glob = c * kb + s_loc
        sel = (key > thr) | ((key == thr) & (s_glob <= j_last))
        sel = sel & jnp.logical_not((c == i) & future)
        mask_scr[rows, :] = jnp.where(sel, 0.0, NEG_BIG)
        return carry
    lax.fori_loop(0, n_blocks, mask_block, 0)

    for g in range(ATT_KV_HEADS):
        qg = qs_scr[g * ATT_GROUP * tq:(g + 1) * ATT_GROUP * tq, :]
        m_scr[...] = jnp.full(m_scr.shape, NEG_BIG, F32)
        l_scr[...] = jnp.zeros_like(l_scr)
        acc_scr[...] = jnp.zeros_like(acc_scr)

        def att_block(c, band):
            rows = pl.ds(pl.multiple_of(c * kb, kb), kb)
            s = _dot_nt(k_ref[0, rows, g * hd:(g + 1) * hd], qg)
            mk = mask_scr[rows, :]
            s = s + jnp.concatenate([mk] * ATT_GROUP, axis=1)
            if band is not None:
                s = s + bias_ref[g, band]
            m_old = m_scr[...]
            m_new = jnp.maximum(m_old, jnp.max(s, axis=0, keepdims=True))
            alpha = jnp.exp(m_old - m_new)
            p = jnp.exp(s - m_new)
            l_scr[...] = alpha * l_scr[...] + jnp.sum(p, axis=0, keepdims=True)
            vt = vt_ref[0, g * hd:(g + 1) * hd, rows]
            acc_scr[...] = alpha * acc_scr[...] + _dot(vt, p.astype(BF16))
            m_scr[...] = m_new

        def far_block(c, carry):
            att_block(c, None)
            return carry
        lax.fori_loop(0, jnp.maximum(i - 1, 0), far_block, 0)

        @pl.when(i >= 1)
        def _():
            att_block(i - 1, 1)
        att_block(i, 0)

        res = acc_scr[...] / l_scr[...]
        for hh in range(ATT_GROUP):
            h = g * ATT_GROUP + hh
            out_scr[h * hd:(h + 1) * hd, :] = res[:, hh * tq:(hh + 1) * tq]

    o_ref[0] = out_scr[...].T.astype(o_ref.dtype)


def _dsa_mixer(h3, gain, w_in, rel_bias):
    B, L, D = h3.shape
    top_k = min(TOPK_MAX, L // 4)
    tm = 512
    nq = ATT_HEADS * ATT_HEAD_DIM
    nkv = ATT_KV_HEADS * ATT_HEAD_DIM
    nqi = IDX_HEADS * IDX_HEAD_DIM
    width = w_in.shape[1]
    padded = nq + 2 * nkv + nqi + 128
    w = jnp.pad(w_in, ((0, 0), (0, padded - width))).astype(BF16)
    q, k, vt, qi, ki, wt = pl.pallas_call(
        _dsa_proj_kernel,
        grid=(B, L // tm),
        in_specs=[
            pl.BlockSpec((1, tm, D), lambda b, r: (b, r, 0)),
            pl.BlockSpec((1, D), lambda b, r: (0, 0)),
            pl.BlockSpec((D, padded), lambda b, r: (0, 0)),
        ],
        out_specs=[
            pl.BlockSpec((1, tm, nq), lambda b, r: (b, r, 0)),
            pl.BlockSpec((1, tm, nkv), lambda b, r: (b, r, 0)),
            pl.BlockSpec((1, nkv, tm), lambda b, r: (b, 0, r)),
            pl.BlockSpec((1, tm, nqi), lambda b, r: (b, r, 0)),
            pl.BlockSpec((1, tm, IDX_HEAD_DIM), lambda b, r: (b, r, 0)),
            pl.BlockSpec((1, IDX_HEADS, tm), lambda b, r: (b, 0, r)),
        ],
        out_shape=[
            jax.ShapeDtypeStruct((B, L, nq), BF16),
            jax.ShapeDtypeStruct((B, L, nkv), BF16),
            jax.ShapeDtypeStruct((B, nkv, L), BF16),
            jax.ShapeDtypeStruct((B, L, nqi), BF16),
            jax.ShapeDtypeStruct((B, L, IDX_HEAD_DIM), BF16),
            jax.ShapeDtypeStruct((B, IDX_HEADS, L), F32),
        ],
        compiler_params=_params(2),
        name="dsa_proj",
    )(h3, gain.reshape(1, D), w)

    bias = pl.pallas_call(
        _bias_kernel,
        in_specs=[pl.BlockSpec(memory_space=pltpu.SMEM)],
        out_specs=pl.BlockSpec(memory_space=pltpu.VMEM),
        out_shape=jax.ShapeDtypeStruct((ATT_KV_HEADS, 2, KEY_BLOCK, ATT_GROUP * KEY_BLOCK), F32),
        name="dsa_bias_tiles",
    )(rel_bias)

    tq = Q_BLOCK
    return pl.pallas_call(
        functools.partial(_dsa_attn_kernel, top_k=top_k, seq=L),
        grid=(B, L // tq),
        in_specs=[
            pl.BlockSpec((1, tq, nq), lambda b, i: (b, i, 0)),
            pl.BlockSpec((1, L, nkv), lambda b, i: (b, 0, 0)),
            pl.BlockSpec((1, nkv, L), lambda b, i: (b, 0, 0)),
            pl.BlockSpec((1, tq, nqi), lambda b, i: (b, i, 0)),
            pl.BlockSpec((1, L, IDX_HEAD_DIM), lambda b, i: (b, 0, 0)),
            pl.BlockSpec((1, IDX_HEADS, tq), lambda b, i: (b, 0, i)),
            pl.BlockSpec((ATT_KV_HEADS, 2, KEY_BLOCK, ATT_GROUP * KEY_BLOCK), lambda b, i: (0, 0, 0, 0)),
        ],
        out_specs=pl.BlockSpec((1, tq, D), lambda b, i: (b, i, 0)),
        out_shape=jax.ShapeDtypeStruct((B, L, D), BF16),
        scratch_shapes=[
            pltpu.VMEM((ATT_HEADS * tq, ATT_HEAD_DIM), BF16),
            pltpu.VMEM((IDX_HEADS * tq, IDX_HEAD_DIM), BF16),
            pltpu.VMEM((L + 2 * KEY_BLOCK, tq), I32),
            pltpu.VMEM((L, tq), F32),
            pltpu.VMEM((1, ATT_GROUP * tq), F32),
            pltpu.VMEM((1, ATT_GROUP * tq), F32),
            pltpu.VMEM((ATT_HEAD_DIM, ATT_GROUP * tq), F32),
            pltpu.VMEM((D, tq), F32),
            pltpu.VMEM((8, tq), I32),
        ],
        compiler_params=_params(2),
        name="dsa_attention",
    )(q, k, vt, qi, ki, wt, bias)


def _ffn_kernel(h_ref, o_ref, wo_ref, gain_ref, wup_ref, cw_ref, cb_ref, wdn_ref, fgain_ref, out_ref,
                xn_scr, hbuf, carry_scr, acc_scr, *, tiles_per_seq, final):
    tm = h_ref.shape[0]
    n_ff, _, two_tn = wup_ref.shape
    tn = two_tn // 2
    i = pl.program_id(0)

    hnew = h_ref[...] + _dot(o_ref[...], wo_ref[...])
    acc_scr[...] = hnew
    xn_scr[...] = _rms_norm(hnew, gain_ref[...]).astype(BF16)
    first = (i % tiles_per_seq) == 0

    def ff_tile(j, carry):
        hu = _dot(xn_scr[...], wup_ref[j])
        @pl.when(first)
        def _():
            hbuf[0:8, :] = jnp.zeros((8, two_tn), F32)

        @pl.when(jnp.logical_not(first))
        def _():
            hbuf[0:8, :] = carry_scr[j]
        hbuf[8:8 + tm, :] = hu
        carry_scr[j] = hu[tm - 8:tm, :]
        cw = cw_ref[j]
        y = (cw[2:3, :] * hbuf[8:8 + tm, :] + cw[1:2, :] * hbuf[7:7 + tm, :]
             + cw[0:1, :] * hbuf[6:6 + tm, :] + cb_ref[j])
        gate = y[:, :tn]
        act = (gate * jax.nn.sigmoid(gate) * y[:, tn:]).astype(BF16)
        acc_scr[...] += _dot(act, wdn_ref[j])
        return carry
    lax.fori_loop(0, n_ff, ff_tile, 0)

    res = acc_scr[...]
    if final:
        res = _rms_norm(res, fgain_ref[...])
    out_ref[...] = res


def _ffn(h2, o2, w_out, gain, w_up, conv_w, conv_b, w_down, final_gain, *, seq, final):
    M, D = h2.shape
    F = w_down.shape[0]
    tm = 512
    tn = 256
    n_ff = F // tn
    assert F % tn == 0 and seq % tm == 0
    wup = w_up.reshape(D, 2, n_ff, tn).transpose(2, 0, 1, 3).reshape(n_ff, D, 2 * tn).astype(BF16)
    cw = conv_w.reshape(3, 2, n_ff, tn).transpose(2, 0, 1, 3).reshape(n_ff, 3, 2 * tn)
    cb = conv_b.reshape(2, n_ff, tn).transpose(1, 0, 2).reshape(n_ff, 1, 2 * tn)
    wdn = w_down.reshape(n_ff, tn, D).astype(BF16)
    const2 = lambda i: (0, 0)
    const3 = lambda i: (0, 0, 0)
    return pl.pallas_call(
        functools.partial(_ffn_kernel, tiles_per_seq=seq // tm, final=final),
        grid=(M // tm,),
        in_specs=[
            pl.BlockSpec((tm, D), lambda i: (i, 0)),
            pl.BlockSpec((tm, D), lambda i: (i, 0)),
            pl.BlockSpec((D, D), const2, pipeline_mode=pl.Buffered(1)),
            pl.BlockSpec((1, D), const2),
            pl.BlockSpec((n_ff, D, 2 * tn), const3, pipeline_mode=pl.Buffered(1)),
            pl.BlockSpec((n_ff, 3, 2 * tn), const3),
            pl.BlockSpec((n_ff, 1, 2 * tn), const3),
            pl.BlockSpec((n_ff, tn, D), const3, pipeline_mode=pl.Buffered(1)),
            pl.BlockSpec((1, D), const2),
        ],
        out_specs=pl.BlockSpec((tm, D), lambda i: (i, 0)),
        out_shape=jax.ShapeDtypeStruct((M, D), F32),
        scratch_shapes=[
            pltpu.VMEM((tm, D), BF16),
            pltpu.VMEM((tm + 8, 2 * tn), F32),
            pltpu.VMEM((n_ff, 8, 2 * tn), F32),
            pltpu.VMEM((tm, D), F32),
        ],
        compiler_params=_params(1),
        name="outproj_convffn",
    )(h2, o2, w_out.astype(BF16), gain.reshape(1, D), wup, cw, cb, wdn, final_gain.reshape(1, D))


def kernel(x, attn_norm, ffn_norm, hgrn_w_in, hgrn_w_out, hgrn_gate_norm, hgrn_lower_bounds, dsa_w_in, dsa_w_out, rel_bias, ffn_w_up, ffn_conv_w, ffn_conv_b, ffn_w_down, final_norm):
    B, L, D = x.shape
    depth = attn_norm.shape[0]
    h = x.reshape(B * L, D)
    for layer in range(depth):
        j = layer // 2
        h3 = h.reshape(B, L, D)
        if layer % 2 == 0:
            o = _hgrn_mixer(h3, attn_norm[layer], hgrn_w_in[j], hgrn_lower_bounds, hgrn_gate_norm[j], j)
            w_out = hgrn_w_out[j]
        else:
            o = _dsa_mixer(h3, attn_norm[layer], dsa_w_in[j], rel_bias)
            w_out = dsa_w_out[j]
        h = _ffn(h, o.reshape(B * L, D), w_out, ffn_norm[layer], ffn_w_up[layer], ffn_conv_w[layer],
                 ffn_conv_b[layer], ffn_w_down[layer], final_norm, seq=L, final=(layer == depth - 1))
    return h.reshape(B, L, D)
```

```python
import functools
import math

import jax
import jax.numpy as jnp
from jax import lax
from jax.experimental import pallas as pl
from jax.experimental.pallas import tpu as pltpu

F32 = jnp.float32
BF16 = jnp.bfloat16
I32 = jnp.int32

EPS = 1e-6
NEG_BIG = -1e30
TINY = 1e-30

HGRN_HEAD_DIM = 128
HGRN_BLOCK = 128
HGRN_HEADS_PER_STEP = 2
ATT_HEADS = 16
ATT_HEAD_DIM = 64
ATT_KV_HEADS = 2
ATT_GROUP = ATT_HEADS // ATT_KV_HEADS
IDX_HEADS = 8
IDX_HEAD_DIM = 64
TOPK_MAX = 256
REL_BUCKETS = 32
REL_MAX_DIST = 128
KEY_BLOCK = 128
KEY_STEP = 2 * KEY_BLOCK
Q_BLOCK = 128
BIAS_TILES = 3
INT_MIN = -(2 ** 31)

VMEM_LIMIT_BYTES = 56 * 1024 * 1024


def _dot(a, b):
    return jnp.dot(a, b, preferred_element_type=F32)


def _dot_nt(a, b):
    return lax.dot_general(a, b, (((1,), (1,)), ((), ())), preferred_element_type=F32)


def _rms_norm(x, gain):
    y = x * lax.rsqrt(jnp.mean(x * x, axis=-1, keepdims=True) + EPS)
    return y * gain


def _params(n_axes):
    return pltpu.CompilerParams(dimension_semantics=("arbitrary",) * n_axes,
                                vmem_limit_bytes=VMEM_LIMIT_BYTES)


def _block_rows(a, m):
    c = a.shape[0]
    return a.reshape(c // m, m, a.shape[1])[:, m - 1:m, :]


def _spread_rows(r, m):
    n = r.shape[0]
    return jnp.broadcast_to(r, (n, m, r.shape[2])).reshape(n * m, r.shape[2])


def _hgrn_kernel(x_ref, gain_ref, w_ref, lbraw_ref, gn_ref, o_ref, xn_scr, proj_scr, st_scr,
                 *, layer_j, seq):
    C = HGRN_BLOCK
    dh = HGRN_HEAD_DIM
    head = pl.program_id(1)

    @pl.when(head == 0)
    def _():
        rb = 256

        def norm_rows(r, carry):
            rows = pl.ds(pl.multiple_of(r * rb, rb), rb)
            xn_scr[rows, :] = _rms_norm(x_ref[0, rows, :], gain_ref[...]).astype(BF16)
            return carry
        lax.fori_loop(0, seq // rb, norm_rows, 0)

    pb = 512

    def proj_rows(r, carry):
        rows = pl.ds(pl.multiple_of(r * pb, pb), pb)
        proj_scr[rows, :] = _dot(xn_scr[rows, :], w_ref[0])
        return carry
    lax.fori_loop(0, seq // pb, proj_rows, 0)

    raw = lbraw_ref[...]
    ex = jnp.exp(raw - jnp.max(raw, axis=0, keepdims=True))
    soft = ex / jnp.sum(ex, axis=0, keepdims=True)
    lb_all = jnp.zeros((1, raw.shape[1]), F32)
    for l in range(1, layer_j + 1):
        lb_all = lb_all + soft[l:l + 1, :]
    gn = gn_ref[...]

    ri = lax.broadcasted_iota(I32, (C, C), 0)
    ci = lax.broadcasted_iota(I32, (C, C), 1)
    tril = (ci <= ri).astype(BF16)
    trow = lax.broadcasted_iota(I32, (C, dh), 0)
    sub = trow & 7

    def same_block(n):
        sh = int(math.log2(n))
        return ((ri >> sh) == (ci >> sh)).astype(F32)

    st_scr[...] = jnp.zeros_like(st_scr)

    def head_step(rows, hh):
        c0 = hh * 4 * dh
        q = proj_scr[rows, c0:c0 + dh]
        f = proj_scr[rows, c0 + dh:c0 + 2 * dh]
        iv = proj_scr[rows, c0 + 2 * dh:c0 + 3 * dh]
        g = proj_scr[rows, c0 + 3 * dh:c0 + 4 * dh]
        lb = lb_all[:, hh * dh:(hh + 1) * dh]
        oml = 1.0 - lb

        e = jnp.exp(-jnp.abs(f))
        r = 1.0 / (1.0 + e)
        er = e * r
        pos = f >= 0.0
        forget = lb + oml * jnp.where(pos, r, er)
        logf = jnp.log(jnp.maximum(forget, TINY))
        k = oml * jnp.where(pos, er, r)

        hi = logf.astype(BF16)
        r1 = logf - hi.astype(F32)
        mid = r1.astype(BF16)
        lo = (r1 - mid.astype(F32)).astype(BF16)
        b = _dot(tril, hi) + _dot(tril, mid) + _dot(tril, lo)

        b8 = b.reshape(C // 8, 8, dh)
        qs, ks = [], []
        for rho in range(8):
            bref = _spread_rows(b8[:, rho:rho + 1, :], 8)
            qs.append(jnp.where(sub >= rho, q * jnp.exp(b - bref), 0.0).astype(BF16))
            ks.append(jnp.where(sub == rho, k, 0.0).astype(BF16))
        acc = _dot_nt(jnp.concatenate(qs, axis=1), jnp.concatenate(ks, axis=1))
        for m in (8, 16, 32, 64):
            last = _block_rows(b, m)
            prev = jnp.concatenate([jnp.zeros((1, 1, dh), F32), last[:-1]], axis=0)
            upper = ((trow >> int(math.log2(m))) & 1) == 1
            qm = jnp.where(upper, q * jnp.exp(b - _spread_rows(prev, m)), 0.0).astype(BF16)
            km = jnp.where(upper, 0.0, k * jnp.exp(_spread_rows(last, m) - b)).astype(BF16)
            acc = _dot_nt(qm, km) + same_block(m) * acc

        st = st_scr[hh]
        o = _dot(acc.astype(BF16), iv.astype(BF16)) + _dot_nt((q * jnp.exp(b)).astype(BF16), st.astype(BF16))
        b_last = b[C - 1:C, :]
        kh = (k * jnp.exp(b_last - b)).astype(BF16)
        st_scr[hh] = st * jnp.exp(b_last) + _dot(iv.T.astype(BF16), kh)

        o = o * lax.rsqrt(jnp.mean(o * o, axis=-1, keepdims=True) + EPS) * gn
        o = o * (g * jax.nn.sigmoid(g))
        o_ref[0, rows, hh * dh:(hh + 1) * dh] = o.astype(o_ref.dtype)

    def step(c, carry):
        rows = pl.ds(pl.multiple_of(c * C, C), C)
        for hh in range(HGRN_HEADS_PER_STEP):
            head_step(rows, hh)
        return carry

    lax.fori_loop(0, seq // C, step, 0)


def _hgrn_mixer(h3, gain, w_in, lower_raw, gate_norm, layer_j):
    B, L, D = h3.shape
    dh = HGRN_HEAD_DIM
    H = D // dh
    n_layers = lower_raw.shape[0]
    hp = HGRN_HEADS_PER_STEP
    assert H % hp == 0
    w = (w_in.reshape(D, 4, H // hp, hp, dh).transpose(2, 0, 3, 1, 4)
         .reshape(H // hp, D, hp * 4 * dh).astype(BF16))
    return pl.pallas_call(
        functools.partial(_hgrn_kernel, layer_j=layer_j, seq=L),
        grid=(B, H // hp),
        in_specs=[
            pl.BlockSpec((1, L, D), lambda b, h: (b, 0, 0)),
            pl.BlockSpec((1, D), lambda b, h: (0, 0)),
            pl.BlockSpec((1, D, hp * 4 * dh), lambda b, h: (h, 0, 0)),
            pl.BlockSpec((n_layers, hp * dh), lambda b, h: (0, h)),
            pl.BlockSpec((1, dh), lambda b, h: (0, 0)),
        ],
        out_specs=pl.BlockSpec((1, L, hp * dh), lambda b, h: (b, 0, h)),
        out_shape=jax.ShapeDtypeStruct((B, L, D), BF16),
        scratch_shapes=[pltpu.VMEM((L, D), BF16), pltpu.VMEM((L, hp * 4 * dh), F32),
                        pltpu.VMEM((hp, dh, dh), F32)],
        compiler_params=_params(2),
        name="hgrn_mixer",
    )(h3, gain.reshape(1, D), w, lower_raw, gate_norm.reshape(1, dh))


def _dsa_proj_kernel(x_ref, gain_ref, w_ref, q_ref, k_ref, vt_ref, qi_ref, ki_ref, wt_ref):
    nq = ATT_HEADS * ATT_HEAD_DIM
    nkv = ATT_KV_HEADS * ATT_HEAD_DIM
    nqi = IDX_HEADS * IDX_HEAD_DIM
    xn = _rms_norm(x_ref[0], gain_ref[...]).astype(BF16)
    p = _dot(xn, w_ref[...])
    c0 = 0
    q_ref[0] = (p[:, c0:c0 + nq] * (ATT_HEAD_DIM ** -0.5)).astype(BF16)
    c0 += nq
    k_ref[0] = p[:, c0:c0 + nkv].astype(BF16)
    c0 += nkv
    vt_ref[0] = p[:, c0:c0 + nkv].T.astype(BF16)
    c0 += nkv
    qi_ref[0] = p[:, c0:c0 + nqi].astype(BF16)
    c0 += nqi
    ki_ref[0] = p[:, c0:c0 + IDX_HEAD_DIM].astype(BF16)
    tail = p[:, c0:c0 + 128].T
    wt_ref[0] = tail[IDX_HEAD_DIM:IDX_HEAD_DIM + IDX_HEADS, :] * (IDX_HEADS ** -0.5 * IDX_HEAD_DIM ** -0.5)


def _bias_kernel(rb_ref, out_ref):
    kb = KEY_BLOCK
    s = lax.broadcasted_iota(I32, (kb, kb), 0)
    t = lax.broadcasted_iota(I32, (kb, kb), 1)
    max_exact = REL_BUCKETS // 2
    for tile in range(2):
        n = jnp.maximum(t - s + kb * tile, 0)
        nf = jnp.maximum(n, max_exact).astype(F32)
        large = max_exact + (jnp.log(nf / max_exact) / math.log(REL_MAX_DIST / max_exact)
                             * (REL_BUCKETS - max_exact)).astype(I32)
        large = jnp.minimum(large, REL_BUCKETS - 1)
        bucket = jnp.where(n < max_exact, n, large)
        for h in range(ATT_HEADS):
            def pick(bk, acc):
                return jnp.where(bucket == bk, rb_ref[bk, h], acc)
            tile_h = lax.fori_loop(0, REL_BUCKETS, pick, jnp.zeros((kb, kb), F32))
            out_ref[h // ATT_GROUP, tile, :, (h % ATT_GROUP) * kb:(h % ATT_GROUP + 1) * kb] = (
                tile_h - rb_ref[REL_BUCKETS - 1, h])
    for g in range(ATT_KV_HEADS):
        out_ref[g, 2] = jnp.zeros((kb, ATT_GROUP * kb), F32)


def _dsa_attn_kernel(q_ref, k_ref, vt_ref, qi_ref, ki_ref, wt_ref, bias_ref, o_ref,
                     qs_scr, qis_scr, key_scr, mask_scr, s_scr, acc_scr, out_scr, thr_scr,
                     *, top_k, seq):
    tq = Q_BLOCK
    kb = KEY_BLOCK
    sb = KEY_STEP
    hd = ATT_HEAD_DIM
    gw = ATT_GROUP * tq
    i = pl.program_id(1)
    n_steps = (i + 2) // 2

    for h in range(ATT_HEADS):
        qs_scr[h * tq:(h + 1) * tq, :] = q_ref[0, :, h * hd:(h + 1) * hd]
    for h in range(IDX_HEADS):
        qis_scr[h * tq:(h + 1) * tq, :] = qi_ref[0, :, h * IDX_HEAD_DIM:(h + 1) * IDX_HEAD_DIM]
    wt = wt_ref[0]

    s_iota = lax.broadcasted_iota(I32, (sb, tq), 0)
    t_glob = i * tq + lax.broadcasted_iota(I32, (sb, tq), 1)

    def step_rows(j):
        return pl.ds(pl.multiple_of(j * sb, sb), sb)

    def fold(x, op):
        return op(x.reshape(sb // 8, 8, x.shape[1]), axis=0)

    def idx_step(j, carry):
        rows = step_rows(j)
        d = _dot_nt(ki_ref[0, rows, :], qis_scr[...])
        sc = jnp.zeros((sb, tq), F32)
        for h in range(IDX_HEADS):
            sc = sc + wt[h:h + 1, :] * jnp.maximum(d[:, h * tq:(h + 1) * tq], 0.0)
        sc = jnp.where(j * sb + s_iota > t_glob, NEG_BIG, sc) + 0.0
        bits = pltpu.bitcast(sc, I32)
        key_scr[rows, :] = jnp.where(bits < 0, bits ^ 0x7FFFFFFF, bits)
        return carry
    lax.fori_loop(0, n_steps, idx_step, 0)

    def count(pred):
        def body(j, cnt):
            hit = pred(key_scr[step_rows(j), :], j * sb + s_iota)
            return cnt + fold(jnp.where(hit, 1, 0), jnp.sum)
        return jnp.sum(lax.fori_loop(0, n_steps, body, jnp.zeros((8, tq), I32)), axis=0, keepdims=True)

    thr_scr[0:1, :] = jnp.full((1, tq), INT_MIN, I32)
    thr_scr[1:2, :] = jnp.full((1, tq), 2 ** 31 - 1, I32)

    @pl.when((i + 1) * kb > top_k)
    def _():
        c0 = count(lambda key, s: key >= 0)
        t0 = jnp.where(c0 >= top_k, 0, INT_MIN).astype(I32)

        def bit_step(it, t_cur):
            cand = t_cur | jnp.left_shift(jnp.int32(1), 30 - it)
            cnt = count(lambda key, s: key >= cand)
            return jnp.where(cnt >= top_k, cand, t_cur)
        thr = lax.fori_loop(0, 31, bit_step, t0)
        thr_scr[0:1, :] = thr

        n_ge = count(lambda key, s: key >= thr)

        @pl.when(jnp.max(n_ge) > top_k)
        def _():
            need = top_k - count(lambda key, s: key > thr)

            pos_bits = (seq - 1).bit_length()

            def pos_step(it, j_cur):
                cand = j_cur | jnp.left_shift(jnp.int32(1), pos_bits - 1 - it)
                cnt = count(lambda key, s: (key == thr) & (s < cand))
                return jnp.where(cnt < need, cand, j_cur)
            thr_scr[1:2, :] = lax.fori_loop(0, pos_bits, pos_step, jnp.zeros((1, tq), I32))

    thr = thr_scr[0:1, :]
    j_last = thr_scr[1:2, :]

    def mask_step(j, carry):
        rows = step_rows(j)
        key = key_scr[rows, :]
        s_glob = j * sb + s_iota
        sel = ((key > thr) | ((key == thr) & (s_glob <= j_last))) & (s_glob <= t_glob)
        mask_scr[rows, :] = jnp.where(sel, 0.0, NEG_BIG)
        return carry
    lax.fori_loop(0, n_steps, mask_step, 0)

    far_steps = jnp.maximum(i - 1, 0) // 2
    for g in range(ATT_KV_HEADS):
        qg = qs_scr[g * gw:(g + 1) * gw, :]

        def scores(j):
            rows = step_rows(j)
            s = _dot_nt(k_ref[0, rows, g * hd:(g + 1) * hd], qg)
            return s + jnp.concatenate([mask_scr[rows, :]] * ATT_GROUP, axis=1)

        def far_step(j, m8):
            s = scores(j)
            s_scr[step_rows(j), :] = s
            return jnp.maximum(m8, fold(s, jnp.max))

        def band_step(j, m8):
            s = scores(j)
            parts = []
            for hf in range(sb // kb):
                c = j * (sb // kb) + hf
                tile = jnp.where(c == i, 0, jnp.where(c == i - 1, 1, 2))
                parts.append(s[hf * kb:(hf + 1) * kb, :] + bias_ref[g, tile])
            s = jnp.concatenate(parts, axis=0)
            s_scr[step_rows(j), :] = s
            return jnp.maximum(m8, fold(s, jnp.max))

        m8 = lax.fori_loop(0, far_steps, far_step, jnp.full((8, gw), NEG_BIG, F32))
        m8 = lax.fori_loop(far_steps, n_steps, band_step, m8)
        m = jnp.max(m8, axis=0, keepdims=True)

        acc_scr[...] = jnp.zeros_like(acc_scr)

        def pv_step(j, l8):
            rows = step_rows(j)
            p = jnp.exp(s_scr[rows, :] - m)
            acc_scr[...] += _dot(vt_ref[0, g * hd:(g + 1) * hd, rows], p.astype(BF16))
            return l8 + fold(p, jnp.sum)
        l8 = lax.fori_loop(0, n_steps, pv_step, jnp.zeros((8, gw), F32))

        res = acc_scr[...] / jnp.sum(l8, axis=0, keepdims=True)
        for hh in range(ATT_GROUP):
            h = g * ATT_GROUP + hh
            out_scr[h * hd:(h + 1) * hd, :] = res[:, hh * tq:(hh + 1) * tq]

    o_ref[0] = out_scr[...].T.astype(o_ref.dtype)


def _dsa_mixer(h3, gain, w_in, rel_bias):
    B, L, D = h3.shape
    top_k = min(TOPK_MAX, L // 4)
    assert L % KEY_STEP == 0 and top_k % KEY_BLOCK == 0
    tm = 512
    nq = ATT_HEADS * ATT_HEAD_DIM
    nkv = ATT_KV_HEADS * ATT_HEAD_DIM
    nqi = IDX_HEADS * IDX_HEAD_DIM
    width = w_in.shape[1]
    padded = nq + 2 * nkv + nqi + 128
    w = jnp.pad(w_in, ((0, 0), (0, padded - width))).astype(BF16)
    q, k, vt, qi, ki, wt = pl.pallas_call(
        _dsa_proj_kernel,
        grid=(B, L // tm),
        in_specs=[
            pl.BlockSpec((1, tm, D), lambda b, r: (b, r, 0)),
            pl.BlockSpec((1, D), lambda b, r: (0, 0)),
            pl.BlockSpec((D, padded), lambda b, r: (0, 0)),
        ],
        out_specs=[
            pl.BlockSpec((1, tm, nq), lambda b, r: (b, r, 0)),
            pl.BlockSpec((1, tm, nkv), lambda b, r: (b, r, 0)),
            pl.BlockSpec((1, nkv, tm), lambda b, r: (b, 0, r)),
            pl.BlockSpec((1, tm, nqi), lambda b, r: (b, r, 0)),
            pl.BlockSpec((1, tm, IDX_HEAD_DIM), lambda b, r: (b, r, 0)),
            pl.BlockSpec((1, IDX_HEADS, tm), lambda b, r: (b, 0, r)),
        ],
        out_shape=[
            jax.ShapeDtypeStruct((B, L, nq), BF16),
            jax.ShapeDtypeStruct((B, L, nkv), BF16),
            jax.ShapeDtypeStruct((B, nkv, L), BF16),
            jax.ShapeDtypeStruct((B, L, nqi), BF16),
            jax.ShapeDtypeStruct((B, L, IDX_HEAD_DIM), BF16),
            jax.ShapeDtypeStruct((B, IDX_HEADS, L), F32),
        ],
        compiler_params=_params(2),
        name="dsa_proj",
    )(h3, gain.reshape(1, D), w)

    bias = pl.pallas_call(
        _bias_kernel,
        in_specs=[pl.BlockSpec(memory_space=pltpu.SMEM)],
        out_specs=pl.BlockSpec(memory_space=pltpu.VMEM),
        out_shape=jax.ShapeDtypeStruct((ATT_KV_HEADS, BIAS_TILES, KEY_BLOCK, ATT_GROUP * KEY_BLOCK), F32),
        name="dsa_bias_tiles",
    )(rel_bias)

    tq = Q_BLOCK
    return pl.pallas_call(
        functools.partial(_dsa_attn_kernel, top_k=top_k, seq=L),
        grid=(B, L // tq),
        in_specs=[
            pl.BlockSpec((1, tq, nq), lambda b, i: (b, i, 0)),
            pl.BlockSpec((1, L, nkv), lambda b, i: (b, 0, 0)),
            pl.BlockSpec((1, nkv, L), lambda b, i: (b, 0, 0)),
            pl.BlockSpec((1, tq, nqi), lambda b, i: (b, i, 0)),
            pl.BlockSpec((1, L, IDX_HEAD_DIM), lambda b, i: (b, 0, 0)),
            pl.BlockSpec((1, IDX_HEADS, tq), lambda b, i: (b, 0, i)),
            pl.BlockSpec((ATT_KV_HEADS, BIAS_TILES, KEY_BLOCK, ATT_GROUP * KEY_BLOCK), lambda b, i: (0, 0, 0, 0)),
        ],
        out_specs=pl.BlockSpec((1, tq, D), lambda b, i: (b, i, 0)),
        out_shape=jax.ShapeDtypeStruct((B, L, D), BF16),
        scratch_shapes=[
            pltpu.VMEM((ATT_HEADS * tq, ATT_HEAD_DIM), BF16),
            pltpu.VMEM((IDX_HEADS * tq, IDX_HEAD_DIM), BF16),
            pltpu.VMEM((L, tq), I32),
            pltpu.VMEM((L, tq), F32),
            pltpu.VMEM((L, ATT_GROUP * tq), F32),
            pltpu.VMEM((ATT_HEAD_DIM, ATT_GROUP * tq), F32),
            pltpu.VMEM((D, tq), F32),
            pltpu.VMEM((8, tq), I32),
        ],
        compiler_params=_params(2),
        name="dsa_attention",
    )(q, k, vt, qi, ki, wt, bias)


def _ffn_kernel(h_ref, o_ref, wo_ref, gain_ref, wup_ref, cw_ref, cb_ref, wdn_ref, fgain_ref, out_ref,
                xn_scr, hbuf, carry_scr, acc_scr, *, tiles_per_seq, final):
    tm = h_ref.shape[0]
    n_ff, _, two_tn = wup_ref.shape
    tn = two_tn // 2
    i = pl.program_id(0)

    hnew = h_ref[...] + _dot(o_ref[...], wo_ref[...])
    acc_scr[...] = hnew
    xn_scr[...] = _rms_norm(hnew, gain_ref[...]).astype(BF16)
    first = (i % tiles_per_seq) == 0

    @pl.when(i == 0)
    def _():
        carry_scr[...] = jnp.zeros_like(carry_scr)

    def up(j, slot):
        hu = _dot(xn_scr[...], wup_ref[j])
        hbuf[slot, 0:8, :] = jnp.where(first, 0.0, carry_scr[j])
        hbuf[slot, 8:8 + tm, :] = hu
        carry_scr[j] = hu[tm - 8:tm, :]

    def down(j, slot):
        cw = cw_ref[j]
        y = (cw[2:3, :] * hbuf[slot, 8:8 + tm, :] + cw[1:2, :] * hbuf[slot, 7:7 + tm, :]
             + cw[0:1, :] * hbuf[slot, 6:6 + tm, :] + cb_ref[j])
        gate = y[:, :tn]
        act = (gate * jax.nn.sigmoid(gate) * y[:, tn:]).astype(BF16)
        acc_scr[...] += _dot(act, wdn_ref[j])

    up(0, 0)

    def ff_pair(p, carry):
        up(2 * p + 1, 1)
        down(2 * p, 0)
        up(2 * p + 2, 0)
        down(2 * p + 1, 1)
        return carry
    lax.fori_loop(0, (n_ff - 1) // 2, ff_pair, 0)
    if n_ff % 2 == 0:
        up(n_ff - 1, 1)
        down(n_ff - 2, 0)
        down(n_ff - 1, 1)
    else:
        down(n_ff - 1, 0)

    res = acc_scr[...]
    if final:
        res = _rms_norm(res, fgain_ref[...])
    out_ref[...] = res


def _ffn(h2, o2, w_out, gain, w_up, conv_w, conv_b, w_down, final_gain, *, seq, final):
    M, D = h2.shape
    F = w_down.shape[0]
    tm = 512
    tn = 256
    n_ff = F // tn
    assert F % tn == 0 and seq % tm == 0
    wup = w_up.reshape(D, 2, n_ff, tn).transpose(2, 0, 1, 3).reshape(n_ff, D, 2 * tn).astype(BF16)
    cw = conv_w.reshape(3, 2, n_ff, tn).transpose(2, 0, 1, 3).reshape(n_ff, 3, 2 * tn)
    cb = conv_b.reshape(2, n_ff, tn).transpose(1, 0, 2).reshape(n_ff, 1, 2 * tn)
    wdn = w_down.reshape(n_ff, tn, D).astype(BF16)
    const2 = lambda i: (0, 0)
    const3 = lambda i: (0, 0, 0)
    return pl.pallas_call(
        functools.partial(_ffn_kernel, tiles_per_seq=seq // tm, final=final),
        grid=(M // tm,),
        in_specs=[
            pl.BlockSpec((tm, D), lambda i: (i, 0)),
            pl.BlockSpec((tm, D), lambda i: (i, 0)),
            pl.BlockSpec((D, D), const2, pipeline_mode=pl.Buffered(1)),
            pl.BlockSpec((1, D), const2),
            pl.BlockSpec((n_ff, D, 2 * tn), const3, pipeline_mode=pl.Buffered(1)),
            pl.BlockSpec((n_ff, 3, 2 * tn), const3),
            pl.BlockSpec((n_ff, 1, 2 * tn), const3),
            pl.BlockSpec((n_ff, tn, D), const3, pipeline_mode=pl.Buffered(1)),
            pl.BlockSpec((1, D), const2),
        ],
        out_specs=pl.BlockSpec((tm, D), lambda i: (i, 0)),
        out_shape=jax.ShapeDtypeStruct((M, D), F32),
        scratch_shapes=[
            pltpu.VMEM((tm, D), BF16),
            pltpu.VMEM((2, tm + 8, 2 * tn), F32),
            pltpu.VMEM((n_ff, 8, 2 * tn), F32),
            pltpu.VMEM((tm, D), F32),
        ],
        compiler_params=_params(1),
        name="outproj_convffn",
    )(h2, o2, w_out.astype(BF16), gain.reshape(1, D), wup, cw, cb, wdn, final_gain.reshape(1, D))


def kernel(x, attn_norm, ffn_norm, hgrn_w_in, hgrn_w_out, hgrn_gate_norm, hgrn_lower_bounds, dsa_w_in, dsa_w_out, rel_bias, ffn_w_up, ffn_conv_w, ffn_conv_b, ffn_w_down, final_norm):
    B, L, D = x.shape
    depth = attn_norm.shape[0]
    h = x.reshape(B * L, D)
    for layer in range(depth):
        j = layer // 2
        h3 = h.reshape(B, L, D)
        if layer % 2 == 0:
            o = _hgrn_mixer(h3, attn_norm[layer], hgrn_w_in[j], hgrn_lower_bounds, hgrn_gate_norm[j], j)
            w_out = hgrn_w_out[j]
        else:
            o = _dsa_mixer(h3, attn_norm[layer], dsa_w_in[j], rel_bias)
            w_out = dsa_w_out[j]
        h = _ffn(h, o.reshape(B * L, D), w_out, ffn_norm[layer], ffn_w_up[layer], ffn_conv_w[layer],
                 ffn_conv_b[layer], ffn_w_down[layer], final_norm, seq=L, final=(layer == depth - 1))
    return h.reshape(B, L, D)
```

```python
import functools
import math

import jax
import jax.numpy as jnp
from jax import lax
from jax.experimental import pallas as pl
from jax.experimental.pallas import tpu as pltpu

F32 = jnp.float32
BF16 = jnp.bfloat16
I32 = jnp.int32
I16 = jnp.int16

EPS = 1e-6
NEG_BIG = -1e30
TINY = 1e-30

HGRN_HEAD_DIM = 128
HGRN_BLOCK = 128
HGRN_HEADS_PER_STEP = 4
ATT_HEADS = 16
ATT_HEAD_DIM = 64
ATT_KV_HEADS = 2
ATT_GROUP = ATT_HEADS // ATT_KV_HEADS
IDX_HEADS = 8
IDX_HEAD_DIM = 64
TOPK_MAX = 256
REL_BUCKETS = 32
REL_MAX_DIST = 128
KEY_BLOCK = 128
Q_BLOCK = 2 * KEY_BLOCK
KEY_STEP = Q_BLOCK
BIAS_TILES = 3
INT_MIN = -(2 ** 31)
LOG2E = 1.4426950408889634

VMEM_LIMIT_BYTES = 56 * 1024 * 1024


def _dot(a, b):
    return jnp.dot(a, b, preferred_element_type=F32)


def _dot_nt(a, b):
    return lax.dot_general(a, b, (((1,), (1,)), ((), ())), preferred_element_type=F32)


def _rms_norm(x, gain):
    y = x * lax.rsqrt(jnp.mean(x * x, axis=-1, keepdims=True) + EPS)
    return y * gain


def _params(n_axes):
    return pltpu.CompilerParams(dimension_semantics=("arbitrary",) * n_axes,
                                vmem_limit_bytes=VMEM_LIMIT_BYTES)


def _block_rows(a, m):
    c = a.shape[0]
    return a.reshape(c // m, m, a.shape[1])[:, m - 1:m, :]


def _spread_rows(r, m):
    n = r.shape[0]
    return jnp.broadcast_to(r, (n, m, r.shape[2])).reshape(n * m, r.shape[2])


def _hgrn_kernel(x_ref, gain_ref, w_ref, lbraw_ref, gn_ref, o_ref, xn_scr, proj_scr, st_scr,
                 *, layer_j, seq):
    C = HGRN_BLOCK
    dh = HGRN_HEAD_DIM
    head = pl.program_id(1)

    @pl.when(head == 0)
    def _():
        rb = 256

        def norm_rows(r, carry):
            rows = pl.ds(pl.multiple_of(r * rb, rb), rb)
            xn_scr[rows, :] = _rms_norm(x_ref[0, rows, :], gain_ref[...]).astype(BF16)
            return carry
        lax.fori_loop(0, seq // rb, norm_rows, 0)

    pb = 512

    def proj_rows(r, carry):
        rows = pl.ds(pl.multiple_of(r * pb, pb), pb)
        proj_scr[rows, :] = _dot(xn_scr[rows, :], w_ref[0])
        return carry
    lax.fori_loop(0, seq // pb, proj_rows, 0)

    raw = lbraw_ref[...]
    ex = jnp.exp(raw - jnp.max(raw, axis=0, keepdims=True))
    soft = ex / jnp.sum(ex, axis=0, keepdims=True)
    lb_all = jnp.zeros((1, raw.shape[1]), F32)
    for l in range(1, layer_j + 1):
        lb_all = lb_all + soft[l:l + 1, :]
    gn = gn_ref[...]

    ri = lax.broadcasted_iota(I32, (C, C), 0)
    ci = lax.broadcasted_iota(I32, (C, C), 1)
    tril = (ci <= ri).astype(BF16)
    trow = lax.broadcasted_iota(I32, (C, dh), 0)
    sub = trow & 7

    def same_block(n):
        sh = int(math.log2(n))
        return ((ri >> sh) == (ci >> sh)).astype(F32)

    st_scr[...] = jnp.zeros_like(st_scr)

    def head_step(rows, hh):
        c0 = hh * 4 * dh
        q = proj_scr[rows, c0:c0 + dh]
        f = proj_scr[rows, c0 + dh:c0 + 2 * dh]
        iv = proj_scr[rows, c0 + 2 * dh:c0 + 3 * dh]
        g = proj_scr[rows, c0 + 3 * dh:c0 + 4 * dh]
        lb = lb_all[:, hh * dh:(hh + 1) * dh]
        oml = 1.0 - lb

        e = jnp.exp(-jnp.abs(f))
        r = 1.0 / (1.0 + e)
        er = e * r
        pos = f >= 0.0
        forget = lb + oml * jnp.where(pos, r, er)
        logf = jnp.log(jnp.maximum(forget, TINY))
        k = oml * jnp.where(pos, er, r)

        hi = logf.astype(BF16)
        r1 = logf - hi.astype(F32)
        mid = r1.astype(BF16)
        lo = (r1 - mid.astype(F32)).astype(BF16)
        b = _dot(tril, hi) + _dot(tril, mid) + _dot(tril, lo)

        b8 = b.reshape(C // 8, 8, dh)
        qs, ks = [], []
        for rho in range(8):
            bref = _spread_rows(b8[:, rho:rho + 1, :], 8)
            qs.append(jnp.where(sub >= rho, q * jnp.exp(b - bref), 0.0).astype(BF16))
            ks.append(jnp.where(sub == rho, k, 0.0).astype(BF16))
        acc = _dot_nt(jnp.concatenate(qs, axis=1), jnp.concatenate(ks, axis=1))
        for m in (8, 16, 32, 64):
            last = _block_rows(b, m)
            prev = jnp.concatenate([jnp.zeros((1, 1, dh), F32), last[:-1]], axis=0)
            upper = ((trow >> int(math.log2(m))) & 1) == 1
            qm = jnp.where(upper, q * jnp.exp(b - _spread_rows(prev, m)), 0.0).astype(BF16)
            km = jnp.where(upper, 0.0, k * jnp.exp(_spread_rows(last, m) - b)).astype(BF16)
            acc = _dot_nt(qm, km) + same_block(m) * acc

        st = st_scr[hh]
        o = _dot(acc.astype(BF16), iv.astype(BF16)) + _dot_nt((q * jnp.exp(b)).astype(BF16), st.astype(BF16))
        b_last = b[C - 1:C, :]
        kh = (k * jnp.exp(b_last - b)).astype(BF16)
        st_scr[hh] = st * jnp.exp(b_last) + _dot(iv.T.astype(BF16), kh)

        o = o * lax.rsqrt(jnp.mean(o * o, axis=-1, keepdims=True) + EPS) * gn
        o = o * (g * jax.nn.sigmoid(g))
        o_ref[0, rows, hh * dh:(hh + 1) * dh] = o.astype(o_ref.dtype)

    def step(c, carry):
        rows = pl.ds(pl.multiple_of(c * C, C), C)
        for hh in range(HGRN_HEADS_PER_STEP):
            head_step(rows, hh)
        return carry

    lax.fori_loop(0, seq // C, step, 0)


def _hgrn_mixer(h3, gain, w_in, lower_raw, gate_norm, layer_j):
    B, L, D = h3.shape
    dh = HGRN_HEAD_DIM
    H = D // dh
    n_layers = lower_raw.shape[0]
    hp = HGRN_HEADS_PER_STEP
    assert H % hp == 0
    w = (w_in.reshape(D, 4, H // hp, hp, dh).transpose(2, 0, 3, 1, 4)
         .reshape(H // hp, D, hp * 4 * dh).astype(BF16))
    return pl.pallas_call(
        functools.partial(_hgrn_kernel, layer_j=layer_j, seq=L),
        grid=(B, H // hp),
        in_specs=[
            pl.BlockSpec((1, L, D), lambda b, h: (b, 0, 0)),
            pl.BlockSpec((1, D), lambda b, h: (0, 0)),
            pl.BlockSpec((1, D, hp * 4 * dh), lambda b, h: (h, 0, 0)),
            pl.BlockSpec((n_layers, hp * dh), lambda b, h: (0, h)),
            pl.BlockSpec((1, dh), lambda b, h: (0, 0)),
        ],
        out_specs=pl.BlockSpec((1, L, hp * dh), lambda b, h: (b, 0, h)),
        out_shape=jax.ShapeDtypeStruct((B, L, D), BF16),
        scratch_shapes=[pltpu.VMEM((L, D), BF16), pltpu.VMEM((L, hp * 4 * dh), F32),
                        pltpu.VMEM((hp, dh, dh), F32)],
        compiler_params=_params(2),
        name="hgrn_mixer",
    )(h3, gain.reshape(1, D), w, lower_raw, gate_norm.reshape(1, dh))


def _dsa_proj_kernel(x_ref, gain_ref, w_ref, q_ref, k_ref, vt_ref, qi_ref, ki_ref, wt_ref):
    nq = ATT_HEADS * ATT_HEAD_DIM
    nkv = ATT_KV_HEADS * ATT_HEAD_DIM
    nqi = IDX_HEADS * IDX_HEAD_DIM
    xn = _rms_norm(x_ref[0], gain_ref[...]).astype(BF16)
    p = _dot(xn, w_ref[...])
    c0 = 0
    q_ref[0] = (p[:, c0:c0 + nq] * (ATT_HEAD_DIM ** -0.5 * LOG2E)).astype(BF16)
    c0 += nq
    k_ref[0] = p[:, c0:c0 + nkv].astype(BF16)
    c0 += nkv
    vt_ref[0] = p[:, c0:c0 + nkv].T.astype(BF16)
    c0 += nkv
    qi_ref[0] = p[:, c0:c0 + nqi].astype(BF16)
    c0 += nqi
    ki_ref[0] = p[:, c0:c0 + IDX_HEAD_DIM].astype(BF16)
    tail = p[:, c0:c0 + 128].T
    wt_ref[0] = tail[IDX_HEAD_DIM:IDX_HEAD_DIM + IDX_HEADS, :] * (IDX_HEADS ** -0.5 * IDX_HEAD_DIM ** -0.5)


def _bias_kernel(rb_ref, out_ref):
    kb = KEY_BLOCK
    s = lax.broadcasted_iota(I32, (kb, kb), 0)
    t = lax.broadcasted_iota(I32, (kb, kb), 1)
    max_exact = REL_BUCKETS // 2
    for tile in range(2):
        n = jnp.maximum(t - s + kb * tile, 0)
        nf = jnp.maximum(n, max_exact).astype(F32)
        large = max_exact + (jnp.log(nf / max_exact) / math.log(REL_MAX_DIST / max_exact)
                             * (REL_BUCKETS - max_exact)).astype(I32)
        large = jnp.minimum(large, REL_BUCKETS - 1)
        bucket = jnp.where(n < max_exact, n, large)
        for h in range(ATT_HEADS):
            def pick(bk, acc):
                return jnp.where(bucket == bk, rb_ref[bk, h], acc)
            tile_h = lax.fori_loop(0, REL_BUCKETS, pick, jnp.zeros((kb, kb), F32))
            out_ref[h // ATT_GROUP, tile, :, (h % ATT_GROUP) * kb:(h % ATT_GROUP + 1) * kb] = (
                (tile_h - rb_ref[REL_BUCKETS - 1, h]) * LOG2E)
    for g in range(ATT_KV_HEADS):
        out_ref[g, 2] = jnp.zeros((kb, ATT_GROUP * kb), F32)


def _dsa_attn_kernel(q_ref, k_ref, vt_ref, qi_ref, ki_ref, wt_ref, bias_ref, o_ref,
                     qs_scr, qis_scr, key_scr, half_scr, mask_scr, s_scr, acc_scr, out_scr, thr_scr,
                     *, top_k, seq):
    tq = Q_BLOCK
    kb = KEY_BLOCK
    sb = KEY_STEP
    nsub = tq // kb
    hd = ATT_HEAD_DIM
    sw = ATT_GROUP * kb
    gw = nsub * sw
    i = pl.program_id(1)
    n_steps = i + 1

    for g in range(ATT_KV_HEADS):
        for th in range(nsub):
            for hh in range(ATT_GROUP):
                r0 = g * gw + th * sw + hh * kb
                c0 = (g * ATT_GROUP + hh) * hd
                qs_scr[r0:r0 + kb, :] = q_ref[0, th * kb:(th + 1) * kb, c0:c0 + hd]
    for h in range(IDX_HEADS):
        qis_scr[h * tq:(h + 1) * tq, :] = qi_ref[0, :, h * IDX_HEAD_DIM:(h + 1) * IDX_HEAD_DIM]
    wt = wt_ref[0]

    s_iota = lax.broadcasted_iota(I32, (sb, tq), 0)
    t_glob = i * tq + lax.broadcasted_iota(I32, (sb, tq), 1)

    def step_rows(j):
        return pl.ds(pl.multiple_of(j * sb, sb), sb)

    def fold(x, op):
        return op(x.reshape(sb // 8, 8, x.shape[1]), axis=0)

    def idx_step(j, carry):
        rows = step_rows(j)
        d = _dot_nt(ki_ref[0, rows, :], qis_scr[...])
        sc = jnp.zeros((sb, tq), F32)
        for h in range(IDX_HEADS):
            sc = sc + wt[h:h + 1, :] * jnp.maximum(d[:, h * tq:(h + 1) * tq], 0.0)
        sc = jnp.where(j * sb + s_iota > t_glob, NEG_BIG, sc) + 0.0
        bits = pltpu.bitcast(sc, I32)
        key = jnp.where(bits < 0, bits ^ 0x7FFFFFFF, bits)
        key_scr[rows, :] = key
        half_scr[rows, :] = (key >> 16).astype(I16)
        return carry
    lax.fori_loop(0, n_steps, idx_step, 0)

    def count(pred):
        def body(j, cnt):
            hit = pred(key_scr[step_rows(j), :], j * sb + s_iota)
            return cnt + fold(jnp.where(hit, 1, 0), jnp.sum)
        return jnp.sum(lax.fori_loop(0, n_steps, body, jnp.zeros((8, tq), I32)), axis=0, keepdims=True)

    def count_half(pred):
        one, nil = jnp.ones((), I16), jnp.zeros((), I16)

        def body(j, cnt):
            hit = pred(half_scr[step_rows(j), :])
            ones = jnp.where(hit, one, nil)
            for r in range(sb // 16):
                cnt = cnt + ones[r * 16:(r + 1) * 16, :]
            return cnt
        cnt = lax.fori_loop(0, n_steps, body, jnp.zeros((16, tq), I16))
        return jnp.sum(cnt.astype(I32), axis=0, keepdims=True)

    def kth_largest_half(target):
        c0 = count_half(lambda v: v >= jnp.zeros((), I16))
        t0 = jnp.where(c0 >= target, 0, -(2 ** 15)).astype(I32)

        def bit_step(it, t_cur):
            cand = t_cur | jnp.left_shift(jnp.int32(1), 14 - it)
            cnt = count_half(lambda v: v >= cand.astype(I16))
            return jnp.where(cnt >= target, cand, t_cur)
        return lax.fori_loop(0, 15, bit_step, t0)

    thr_scr[0:1, :] = jnp.full((1, tq), INT_MIN, I32)
    thr_scr[1:2, :] = jnp.full((1, tq), 2 ** 31 - 1, I32)

    @pl.when((i + 1) * tq > top_k)
    def _():
        t_hi = kth_largest_half(top_k)
        need_lo = top_k - count_half(lambda v: v > t_hi.astype(I16))

        def low_step(j, carry):
            rows = step_rows(j)
            key = key_scr[rows, :]
            low = (key & 0xFFFF) - 2 ** 15
            half_scr[rows, :] = jnp.where((key >> 16) == t_hi, low, -(2 ** 15)).astype(I16)
            return carry
        lax.fori_loop(0, n_steps, low_step, 0)
        t_lo = kth_largest_half(need_lo)
        thr = (t_hi << 16) | (t_lo + 2 ** 15)
        thr_scr[0:1, :] = thr

        n_ge = count(lambda key, s: key >= thr)

        @pl.when(jnp.max(n_ge) > top_k)
        def _():
            need = top_k - count(lambda key, s: key > thr)

            pos_bits = (seq - 1).bit_length()

            def pos_step(it, j_cur):
                cand = j_cur | jnp.left_shift(jnp.int32(1), pos_bits - 1 - it)
                cnt = count(lambda key, s: (key == thr) & (s < cand))
                return jnp.where(cnt < need, cand, j_cur)
            thr_scr[1:2, :] = lax.fori_loop(0, pos_bits, pos_step, jnp.zeros((1, tq), I32))

    thr = thr_scr[0:1, :]
    j_last = thr_scr[1:2, :]

    def mask_step(j, carry):
        rows = step_rows(j)
        key = key_scr[rows, :]
        s_glob = j * sb + s_iota
        sel = ((key > thr) | ((key == thr) & (s_glob <= j_last))) & (s_glob <= t_glob)
        mask_scr[rows, :] = jnp.where(sel, 0.0, NEG_BIG)
        return carry
    lax.fori_loop(0, n_steps, mask_step, 0)

    far_steps = jnp.maximum(i - 1, 0)
    diag_tiles = ((0, 1), (2, 0))
    prev_tiles = ((2, 2), (1, 2))

    def logits(g, j, band):
        rows = step_rows(j)
        s = _dot_nt(k_ref[0, rows, g * hd:(g + 1) * hd], qs_scr[g * gw:(g + 1) * gw, :])
        mk = mask_scr[rows, :]
        s = s + jnp.concatenate([mk[:, th * kb:(th + 1) * kb] for th in range(nsub) for _ in range(ATT_GROUP)],
                                axis=1)
        if band:
            s = jnp.concatenate([
                jnp.concatenate([
                    s[sh * kb:(sh + 1) * kb, th * sw:(th + 1) * sw]
                    + bias_ref[g, jnp.where(j == i, diag_tiles[sh][th], prev_tiles[sh][th])]
                    for th in range(nsub)], axis=1)
                for sh in range(nsub)], axis=0)
        s_scr[rows, :] = s
        return fold(s, jnp.max)

    def exp_pv(g, j, m, l8):
        rows = step_rows(j)
        p = jnp.exp2(s_scr[rows, :] - m)
        acc_scr[...] += _dot(vt_ref[0, g * hd:(g + 1) * hd, rows], p.astype(BF16))
        return l8 + fold(p, jnp.sum)

    def finish(g, l8):
        res = acc_scr[...] / jnp.sum(l8, axis=0, keepdims=True)
        for th in range(nsub):
            for hh in range(ATT_GROUP):
                h = g * ATT_GROUP + hh
                out_scr[h * hd:(h + 1) * hd, th * kb:(th + 1) * kb] = res[:, th * sw + hh * kb:th * sw + (hh + 1) * kb]

    neg = jnp.full((8, gw), NEG_BIG, F32)
    zero = jnp.zeros((8, gw), F32)
    m8 = lax.fori_loop(0, far_steps, lambda j, c: jnp.maximum(c, logits(0, j, False)), neg)
    m8 = lax.fori_loop(far_steps, n_steps, lambda j, c: jnp.maximum(c, logits(0, j, True)), m8)
    for g in range(ATT_KV_HEADS):
        m = jnp.max(m8, axis=0, keepdims=True)
        acc_scr[...] = jnp.zeros_like(acc_scr)
        if g + 1 < ATT_KV_HEADS:
            def both(j, c, band):
                l8, m8n = c
                l8 = exp_pv(g, j, m, l8)
                return l8, jnp.maximum(m8n, logits(g + 1, j, band))
            l8, m8 = lax.fori_loop(0, far_steps, functools.partial(both, band=False), (zero, neg))
            l8, m8 = lax.fori_loop(far_steps, n_steps, functools.partial(both, band=True), (l8, m8))
        else:
            l8 = lax.fori_loop(0, n_steps, lambda j, c: exp_pv(g, j, m, c), zero)
        finish(g, l8)

    o_ref[0] = out_scr[...].T.astype(o_ref.dtype)


def _dsa_mixer(h3, gain, w_in, rel_bias):
    B, L, D = h3.shape
    top_k = min(TOPK_MAX, L // 4)
    assert L % KEY_STEP == 0 and top_k % KEY_BLOCK == 0
    tm = 512
    nq = ATT_HEADS * ATT_HEAD_DIM
    nkv = ATT_KV_HEADS * ATT_HEAD_DIM
    nqi = IDX_HEADS * IDX_HEAD_DIM
    width = w_in.shape[1]
    padded = nq + 2 * nkv + nqi + 128
    w = jnp.pad(w_in, ((0, 0), (0, padded - width))).astype(BF16)
    q, k, vt, qi, ki, wt = pl.pallas_call(
        _dsa_proj_kernel,
        grid=(B, L // tm),
        in_specs=[
            pl.BlockSpec((1, tm, D), lambda b, r: (b, r, 0)),
            pl.BlockSpec((1, D), lambda b, r: (0, 0)),
            pl.BlockSpec((D, padded), lambda b, r: (0, 0)),
        ],
        out_specs=[
            pl.BlockSpec((1, tm, nq), lambda b, r: (b, r, 0)),
            pl.BlockSpec((1, tm, nkv), lambda b, r: (b, r, 0)),
            pl.BlockSpec((1, nkv, tm), lambda b, r: (b, 0, r)),
            pl.BlockSpec((1, tm, nqi), lambda b, r: (b, r, 0)),
            pl.BlockSpec((1, tm, IDX_HEAD_DIM), lambda b, r: (b, r, 0)),
            pl.BlockSpec((1, IDX_HEADS, tm), lambda b, r: (b, 0, r)),
        ],
        out_shape=[
            jax.ShapeDtypeStruct((B, L, nq), BF16),
            jax.ShapeDtypeStruct((B, L, nkv), BF16),
            jax.ShapeDtypeStruct((B, nkv, L), BF16),
            jax.ShapeDtypeStruct((B, L, nqi), BF16),
            jax.ShapeDtypeStruct((B, L, IDX_HEAD_DIM), BF16),
            jax.ShapeDtypeStruct((B, IDX_HEADS, L), F32),
        ],
        compiler_params=_params(2),
        name="dsa_proj",
    )(h3, gain.reshape(1, D), w)

    bias = pl.pallas_call(
        _bias_kernel,
        in_specs=[pl.BlockSpec(memory_space=pltpu.SMEM)],
        out_specs=pl.BlockSpec(memory_space=pltpu.VMEM),
        out_shape=jax.ShapeDtypeStruct((ATT_KV_HEADS, BIAS_TILES, KEY_BLOCK, ATT_GROUP * KEY_BLOCK), F32),
        name="dsa_bias_tiles",
    )(rel_bias)

    tq = Q_BLOCK
    return pl.pallas_call(
        functools.partial(_dsa_attn_kernel, top_k=top_k, seq=L),
        grid=(B, L // tq),
        in_specs=[
            pl.BlockSpec((1, tq, nq), lambda b, i: (b, i, 0)),
            pl.BlockSpec((1, L, nkv), lambda b, i: (b, 0, 0)),
            pl.BlockSpec((1, nkv, L), lambda b, i: (b, 0, 0)),
            pl.BlockSpec((1, tq, nqi), lambda b, i: (b, i, 0)),
            pl.BlockSpec((1, L, IDX_HEAD_DIM), lambda b, i: (b, 0, 0)),
            pl.BlockSpec((1, IDX_HEADS, tq), lambda b, i: (b, 0, i)),
            pl.BlockSpec((ATT_KV_HEADS, BIAS_TILES, KEY_BLOCK, ATT_GROUP * KEY_BLOCK), lambda b, i: (0, 0, 0, 0)),
        ],
        out_specs=pl.BlockSpec((1, tq, D), lambda b, i: (b, i, 0)),
        out_shape=jax.ShapeDtypeStruct((B, L, D), BF16),
        scratch_shapes=[
            pltpu.VMEM((ATT_HEADS * tq, ATT_HEAD_DIM), BF16),
            pltpu.VMEM((IDX_HEADS * tq, IDX_HEAD_DIM), BF16),
            pltpu.VMEM((L, tq), I32),
            pltpu.VMEM((L, tq), I16),
            pltpu.VMEM((L, tq), F32),
            pltpu.VMEM((L, ATT_GROUP * tq), F32),
            pltpu.VMEM((ATT_HEAD_DIM, ATT_GROUP * tq), F32),
            pltpu.VMEM((D, tq), F32),
            pltpu.VMEM((8, tq), I32),
        ],
        compiler_params=_params(2),
        name="dsa_attention",
    )(q, k, vt, qi, ki, wt, bias)


def _ffn_kernel(h_ref, o_ref, wo_ref, gain_ref, wup_ref, cw_ref, cb_ref, wdn_ref, fgain_ref, out_ref,
                xn_scr, hbuf, carry_scr, acc_scr, *, tiles_per_seq, final):
    tm = h_ref.shape[0]
    n_ff, _, two_tn = wup_ref.shape
    tn = two_tn // 2
    i = pl.program_id(0)

    hnew = h_ref[...] + _dot(o_ref[...], wo_ref[...])
    acc_scr[...] = hnew
    xn_scr[...] = _rms_norm(hnew, gain_ref[...]).astype(BF16)
    first = (i % tiles_per_seq) == 0

    @pl.when(i == 0)
    def _():
        carry_scr[...] = jnp.zeros_like(carry_scr)

    def up(j, slot):
        hu = _dot(xn_scr[...], wup_ref[j])
        hbuf[slot, 0:8, :] = jnp.where(first, 0.0, carry_scr[j])
        hbuf[slot, 8:8 + tm, :] = hu
        carry_scr[j] = hu[tm - 8:tm, :]

    def down(j, slot):
        cw = cw_ref[j]
        y = (cw[2:3, :] * hbuf[slot, 8:8 + tm, :] + cw[1:2, :] * hbuf[slot, 7:7 + tm, :]
             + cw[0:1, :] * hbuf[slot, 6:6 + tm, :] + cb_ref[j])
        gate = y[:, :tn]
        act = (gate * jax.nn.sigmoid(gate) * y[:, tn:]).astype(BF16)
        acc_scr[...] += _dot(act, wdn_ref[j])

    up(0, 0)

    def ff_pair(p, carry):
        up(2 * p + 1, 1)
        down(2 * p, 0)
        up(2 * p + 2, 0)
        down(2 * p + 1, 1)
        return carry
    lax.fori_loop(0, (n_ff - 1) // 2, ff_pair, 0)
    if n_ff % 2 == 0:
        up(n_ff - 1, 1)
        down(n_ff - 2, 0)
        down(n_ff - 1, 1)
    else:
        down(n_ff - 1, 0)

    res = acc_scr[...]
    if final:
        res = _rms_norm(res, fgain_ref[...])
    out_ref[...] = res


def _ffn(h2, o2, w_out, gain, w_up, conv_w, conv_b, w_down, final_gain, *, seq, final):
    M, D = h2.shape
    F = w_down.shape[0]
    tm = 512
    tn = 256
    n_ff = F // tn
    assert F % tn == 0 and seq % tm == 0
    wup = w_up.reshape(D, 2, n_ff, tn).transpose(2, 0, 1, 3).reshape(n_ff, D, 2 * tn).astype(BF16)
    cw = conv_w.reshape(3, 2, n_ff, tn).transpose(2, 0, 1, 3).reshape(n_ff, 3, 2 * tn)
    cb = conv_b.reshape(2, n_ff, tn).transpose(1, 0, 2).reshape(n_ff, 1, 2 * tn)
    wdn = w_down.reshape(n_ff, tn, D).astype(BF16)
    const2 = lambda i: (0, 0)
    const3 = lambda i: (0, 0, 0)
    return pl.pallas_call(
        functools.partial(_ffn_kernel, tiles_per_seq=seq // tm, final=final),
        grid=(M // tm,),
        in_specs=[
            pl.BlockSpec((tm, D), lambda i: (i, 0)),
            pl.BlockSpec((tm, D), lambda i: (i, 0)),
            pl.BlockSpec((D, D), const2, pipeline_mode=pl.Buffered(1)),
            pl.BlockSpec((1, D), const2),
            pl.BlockSpec((n_ff, D, 2 * tn), const3, pipeline_mode=pl.Buffered(1)),
            pl.BlockSpec((n_ff, 3, 2 * tn), const3),
            pl.BlockSpec((n_ff, 1, 2 * tn), const3),
            pl.BlockSpec((n_ff, tn, D), const3, pipeline_mode=pl.Buffered(1)),
            pl.BlockSpec((1, D), const2),
        ],
        out_specs=pl.BlockSpec((tm, D), lambda i: (i, 0)),
        out_shape=jax.ShapeDtypeStruct((M, D), F32),
        scratch_shapes=[
            pltpu.VMEM((tm, D), BF16),
            pltpu.VMEM((2, tm + 8, 2 * tn), F32),
            pltpu.VMEM((n_ff, 8, 2 * tn), F32),
            pltpu.VMEM((tm, D), F32),
        ],
        compiler_params=_params(1),
        name="outproj_convffn",
    )(h2, o2, w_out.astype(BF16), gain.reshape(1, D), wup, cw, cb, wdn, final_gain.reshape(1, D))


def kernel(x, attn_norm, ffn_norm, hgrn_w_in, hgrn_w_out, hgrn_gate_norm, hgrn_lower_bounds, dsa_w_in, dsa_w_out, rel_bias, ffn_w_up, ffn_conv_w, ffn_conv_b, ffn_w_down, final_norm):
    B, L, D = x.shape
    depth = attn_norm.shape[0]
    h = x.reshape(B * L, D)
    for layer in range(depth):
        j = layer // 2
        h3 = h.reshape(B, L, D)
        if layer % 2 == 0:
            o = _hgrn_mixer(h3, attn_norm[layer], hgrn_w_in[j], hgrn_lower_bounds, hgrn_gate_norm[j], j)
            w_out = hgrn_w_out[j]
        else:
            o = _dsa_mixer(h3, attn_norm[layer], dsa_w_in[j], rel_bias)
            w_out = dsa_w_out[j]
        h = _ffn(h, o.reshape(B * L, D), w_out, ffn_norm[layer], ffn_w_up[layer], ffn_conv_w[layer],
                 ffn_conv_b[layer], ffn_w_down[layer], final_norm, seq=L, final=(layer == depth - 1))
    return h.reshape(B, L, D)
```

```python
import functools
import math

import jax
import jax.numpy as jnp
from jax import lax
from jax.experimental import pallas as pl
from jax.experimental.pallas import tpu as pltpu

F32 = jnp.float32
BF16 = jnp.bfloat16
I32 = jnp.int32
I16 = jnp.int16

EPS = 1e-6
NEG_BIG = -1e30
TINY = 1e-30

HGRN_HEAD_DIM = 128
HGRN_BLOCK = 128
HGRN_HEADS_PER_STEP = 4
ATT_HEADS = 16
ATT_HEAD_DIM = 64
ATT_KV_HEADS = 2
ATT_GROUP = ATT_HEADS // ATT_KV_HEADS
IDX_HEADS = 8
IDX_HEAD_DIM = 64
TOPK_MAX = 256
REL_BUCKETS = 32
REL_MAX_DIST = 128
KEY_BLOCK = 128
Q_BLOCK = 2 * KEY_BLOCK
KEY_STEP = Q_BLOCK
BIAS_TILES = 3
FFN_ROWS = 512
FFN_TILE = 256
INT_MIN = -(2 ** 31)
LOG2E = 1.4426950408889634

VMEM_LIMIT_BYTES = 56 * 1024 * 1024


def _dot(a, b):
    return jnp.dot(a, b, preferred_element_type=F32)


def _dot_nt(a, b):
    return lax.dot_general(a, b, (((1,), (1,)), ((), ())), preferred_element_type=F32)


def _rms_norm(x, gain):
    y = x * lax.rsqrt(jnp.mean(x * x, axis=-1, keepdims=True) + EPS)
    return y * gain


def _pair_loop(lo, hi, body, init):
    def two(p, c):
        j = lo + 2 * p
        return body(j + 1, body(j, c))
    c = lax.fori_loop(0, (hi - lo) // 2, two, init)
    return lax.cond((hi - lo) % 2 == 1, lambda c: body(hi - 1, c), lambda c: c, c)


def _params(n_axes):
    return pltpu.CompilerParams(dimension_semantics=("arbitrary",) * n_axes,
                                vmem_limit_bytes=VMEM_LIMIT_BYTES)


def _block_rows(a, m):
    c = a.shape[0]
    return a.reshape(c // m, m, a.shape[1])[:, m - 1:m, :]


def _spread_rows(r, m):
    n = r.shape[0]
    return jnp.broadcast_to(r, (n, m, r.shape[2])).reshape(n * m, r.shape[2])


def _hgrn_kernel(x_ref, gain_ref, wq_ref, wf_ref, wi_ref, wg_ref, lbraw_ref, gn_ref, o_ref,
                 xn_scr, proj_scr, st_scr, *, layer_j, seq):
    C = HGRN_BLOCK
    dh = HGRN_HEAD_DIM
    pw = HGRN_HEADS_PER_STEP * dh
    head = pl.program_id(1)

    @pl.when(head == 0)
    def _():
        rb = 256

        def norm_rows(r, carry):
            rows = pl.ds(pl.multiple_of(r * rb, rb), rb)
            xn_scr[rows, :] = _rms_norm(x_ref[0, rows, :], gain_ref[...]).astype(BF16)
            return carry
        lax.fori_loop(0, seq // rb, norm_rows, 0)

    pb = 512

    def proj_rows(r, carry):
        rows = pl.ds(pl.multiple_of(r * pb, pb), pb)
        xr = xn_scr[rows, :]
        for part, w_ref in enumerate((wq_ref, wf_ref, wi_ref, wg_ref)):
            proj_scr[rows, part * pw:(part + 1) * pw] = _dot(xr, w_ref[...])
        return carry
    lax.fori_loop(0, seq // pb, proj_rows, 0)

    raw = lbraw_ref[...]
    ex = jnp.exp(raw - jnp.max(raw, axis=0, keepdims=True))
    soft = ex / jnp.sum(ex, axis=0, keepdims=True)
    lb_all = jnp.zeros((1, raw.shape[1]), F32)
    for l in range(1, layer_j + 1):
        lb_all = lb_all + soft[l:l + 1, :]
    gn = gn_ref[...]

    ri = lax.broadcasted_iota(I32, (C, C), 0)
    ci = lax.broadcasted_iota(I32, (C, C), 1)
    tril = (ci <= ri).astype(BF16)
    trow = lax.broadcasted_iota(I32, (C, dh), 0)
    sub = trow & 7

    def same_block(n):
        sh = int(math.log2(n))
        return ((ri >> sh) == (ci >> sh)).astype(F32)

    st_scr[...] = jnp.zeros_like(st_scr)

    def head_step(rows, hh):
        c0 = hh * dh
        q = proj_scr[rows, c0:c0 + dh]
        f = proj_scr[rows, pw + c0:pw + c0 + dh]
        iv = proj_scr[rows, 2 * pw + c0:2 * pw + c0 + dh]
        g = proj_scr[rows, 3 * pw + c0:3 * pw + c0 + dh]
        lb = lb_all[:, hh * dh:(hh + 1) * dh]
        oml = 1.0 - lb

        e = jnp.exp(-jnp.abs(f))
        r = 1.0 / (1.0 + e)
        er = e * r
        pos = f >= 0.0
        forget = lb + oml * jnp.where(pos, r, er)
        logf = jnp.log(jnp.maximum(forget, TINY)) * LOG2E
        k = oml * jnp.where(pos, er, r)

        hi = logf.astype(BF16)
        r1 = logf - hi.astype(F32)
        mid = r1.astype(BF16)
        lo = (r1 - mid.astype(F32)).astype(BF16)
        b = _dot(tril, hi) + _dot(tril, mid) + _dot(tril, lo)

        b8 = b.reshape(C // 8, 8, dh)
        qs, ks = [], []
        for rho in range(8):
            bref = _spread_rows(b8[:, rho:rho + 1, :], 8)
            qs.append(jnp.where(sub >= rho, q * jnp.exp2(b - bref), 0.0).astype(BF16))
            ks.append(jnp.where(sub == rho, k, 0.0).astype(BF16))
        acc = _dot_nt(jnp.concatenate(qs, axis=1), jnp.concatenate(ks, axis=1))
        for m in (8, 16, 32, 64):
            last = _block_rows(b, m)
            prev = jnp.concatenate([jnp.zeros((1, 1, dh), F32), last[:-1]], axis=0)
            upper = ((trow >> int(math.log2(m))) & 1) == 1
            qm = jnp.where(upper, q * jnp.exp2(b - _spread_rows(prev, m)), 0.0).astype(BF16)
            km = jnp.where(upper, 0.0, k * jnp.exp2(_spread_rows(last, m) - b)).astype(BF16)
            acc = _dot_nt(qm, km) + same_block(m) * acc

        st = st_scr[hh]
        o = _dot(acc.astype(BF16), iv.astype(BF16)) + _dot_nt((q * jnp.exp2(b)).astype(BF16), st.astype(BF16))
        b_last = b[C - 1:C, :]
        kh = (k * jnp.exp2(b_last - b)).astype(BF16)
        st_scr[hh] = st * jnp.exp2(b_last) + _dot(iv.T.astype(BF16), kh)

        o = o * lax.rsqrt(jnp.mean(o * o, axis=-1, keepdims=True) + EPS) * gn
        o = o * (g * jax.nn.sigmoid(g))
        o_ref[0, rows, hh * dh:(hh + 1) * dh] = o.astype(o_ref.dtype)

    def step(c, carry):
        rows = pl.ds(pl.multiple_of(c * C, C), C)
        for hh in range(HGRN_HEADS_PER_STEP):
            head_step(rows, hh)
        return carry

    lax.fori_loop(0, seq // C, step, 0)


def _hgrn_mixer(h3, gain, w_in, lower_raw, gate_norm, layer_j):
    B, L, D = h3.shape
    dh = HGRN_HEAD_DIM
    H = D // dh
    n_layers = lower_raw.shape[0]
    hp = HGRN_HEADS_PER_STEP
    assert H % hp == 0
    steps = H // hp
    w = w_in.astype(BF16)

    def part_spec(part):
        return pl.BlockSpec((D, hp * dh), lambda b, h: (0, part * steps + h))
    return pl.pallas_call(
        functools.partial(_hgrn_kernel, layer_j=layer_j, seq=L),
        grid=(B, steps),
        in_specs=[
            pl.BlockSpec((1, L, D), lambda b, h: (b, 0, 0)),
            pl.BlockSpec((1, D), lambda b, h: (0, 0)),
            part_spec(0), part_spec(1), part_spec(2), part_spec(3),
            pl.BlockSpec((n_layers, hp * dh), lambda b, h: (0, h)),
            pl.BlockSpec((1, dh), lambda b, h: (0, 0)),
        ],
        out_specs=pl.BlockSpec((1, L, hp * dh), lambda b, h: (b, 0, h)),
        out_shape=jax.ShapeDtypeStruct((B, L, D), BF16),
        scratch_shapes=[pltpu.VMEM((L, D), BF16), pltpu.VMEM((L, hp * 4 * dh), F32),
                        pltpu.VMEM((hp, dh, dh), F32)],
        compiler_params=_params(2),
        name="hgrn_mixer",
    )(h3, gain.reshape(1, D), w, w, w, w, lower_raw, gate_norm.reshape(1, dh))


def _dsa_proj_kernel(x_ref, gain_ref, w_ref, q_ref, k_ref, vt_ref, qi_ref, ki_ref, wt_ref):
    nq = ATT_HEADS * ATT_HEAD_DIM
    nkv = ATT_KV_HEADS * ATT_HEAD_DIM
    nqi = IDX_HEADS * IDX_HEAD_DIM
    xn = _rms_norm(x_ref[0], gain_ref[...]).astype(BF16)
    p = _dot(xn, w_ref[...])
    c0 = 0
    q_ref[0] = (p[:, c0:c0 + nq] * (ATT_HEAD_DIM ** -0.5 * LOG2E)).astype(BF16)
    c0 += nq
    k_ref[0] = p[:, c0:c0 + nkv].astype(BF16)
    c0 += nkv
    vt_ref[0] = p[:, c0:c0 + nkv].T.astype(BF16)
    c0 += nkv
    qi_ref[0] = p[:, c0:c0 + nqi].astype(BF16)
    c0 += nqi
    ki_ref[0] = p[:, c0:c0 + IDX_HEAD_DIM].astype(BF16)
    tail = p[:, c0:c0 + 128].T
    wt_ref[0] = tail[IDX_HEAD_DIM:IDX_HEAD_DIM + IDX_HEADS, :] * (IDX_HEADS ** -0.5 * IDX_HEAD_DIM ** -0.5)


def _bias_kernel(rb_ref, out_ref):
    kb = KEY_BLOCK
    s = lax.broadcasted_iota(I32, (kb, kb), 0)
    t = lax.broadcasted_iota(I32, (kb, kb), 1)
    max_exact = REL_BUCKETS // 2
    for tile in range(2):
        n = jnp.maximum(t - s + kb * tile, 0)
        nf = jnp.maximum(n, max_exact).astype(F32)
        large = max_exact + (jnp.log(nf / max_exact) / math.log(REL_MAX_DIST / max_exact)
                             * (REL_BUCKETS - max_exact)).astype(I32)
        large = jnp.minimum(large, REL_BUCKETS - 1)
        bucket = jnp.where(n < max_exact, n, large)
        for h in range(ATT_HEADS):
            def pick(bk, acc):
                return jnp.where(bucket == bk, rb_ref[bk, h], acc)
            tile_h = lax.fori_loop(0, REL_BUCKETS, pick, jnp.zeros((kb, kb), F32))
            out_ref[h // ATT_GROUP, tile, :, (h % ATT_GROUP) * kb:(h % ATT_GROUP + 1) * kb] = (
                (tile_h - rb_ref[REL_BUCKETS - 1, h]) * LOG2E)
    for g in range(ATT_KV_HEADS):
        out_ref[g, 2] = jnp.zeros((kb, ATT_GROUP * kb), F32)


def _dsa_attn_kernel(q_ref, k_ref, vt_ref, qi_ref, ki_ref, wt_ref, bias_ref, o_ref,
                     qs_scr, qis_scr, key_scr, half_scr, mask_scr, s_scr, acc_scr, out_scr, thr_scr,
                     *, top_k, seq):
    tq = Q_BLOCK
    kb = KEY_BLOCK
    sb = KEY_STEP
    nsub = tq // kb
    hd = ATT_HEAD_DIM
    sw = ATT_GROUP * kb
    gw = nsub * sw
    i = pl.program_id(1)
    n_steps = i + 1

    for g in range(ATT_KV_HEADS):
        for th in range(nsub):
            for hh in range(ATT_GROUP):
                r0 = g * gw + th * sw + hh * kb
                c0 = (g * ATT_GROUP + hh) * hd
                qs_scr[r0:r0 + kb, :] = q_ref[0, th * kb:(th + 1) * kb, c0:c0 + hd]
    for h in range(IDX_HEADS):
        qis_scr[h * tq:(h + 1) * tq, :] = qi_ref[0, :, h * IDX_HEAD_DIM:(h + 1) * IDX_HEAD_DIM]
    wt = wt_ref[0]

    s_iota = lax.broadcasted_iota(I32, (sb, tq), 0)
    t_glob = i * tq + lax.broadcasted_iota(I32, (sb, tq), 1)

    def step_rows(j):
        return pl.ds(pl.multiple_of(j * sb, sb), sb)

    def fold(x, op):
        return op(x.reshape(sb // 8, 8, x.shape[1]), axis=0)

    def idx_step(j, carry):
        rows = step_rows(j)
        d = _dot_nt(ki_ref[0, rows, :], qis_scr[...])
        sc = jnp.zeros((sb, tq), F32)
        for h in range(IDX_HEADS):
            sc = sc + wt[h:h + 1, :] * jnp.maximum(d[:, h * tq:(h + 1) * tq], 0.0)
        sc = jnp.where(j * sb + s_iota > t_glob, NEG_BIG, sc) + 0.0
        bits = pltpu.bitcast(sc, I32)
        key = jnp.where(bits < 0, bits ^ 0x7FFFFFFF, bits)
        key_scr[rows, :] = key
        half_scr[rows, :] = (key >> 16).astype(I16)
        return carry
    _pair_loop(0, n_steps, idx_step, 0)

    def count(pred):
        def body(j, cnt):
            hit = pred(key_scr[step_rows(j), :], j * sb + s_iota)
            return cnt + fold(jnp.where(hit, 1, 0), jnp.sum)
        return jnp.sum(_pair_loop(0, n_steps, body, jnp.zeros((8, tq), I32)), axis=0, keepdims=True)

    def count_half(pred):
        one, nil = jnp.ones((), I16), jnp.zeros((), I16)

        def body(j, cnt):
            hit = pred(half_scr[step_rows(j), :])
            ones = jnp.where(hit, one, nil)
            for r in range(sb // 16):
                cnt = cnt + ones[r * 16:(r + 1) * 16, :]
            return cnt
        cnt = _pair_loop(0, n_steps, body, jnp.zeros((16, tq), I16))
        return jnp.sum(cnt.astype(I32), axis=0, keepdims=True)

    def kth_largest_half(target):
        c0 = count_half(lambda v: v >= jnp.zeros((), I16))
        t0 = jnp.where(c0 >= target, 0, -(2 ** 15)).astype(I32)

        def bit_step(it, t_cur):
            cand = t_cur | jnp.left_shift(jnp.int32(1), 14 - it)
            cnt = count_half(lambda v: v >= cand.astype(I16))
            return jnp.where(cnt >= target, cand, t_cur)
        return lax.fori_loop(0, 15, bit_step, t0)

    thr_scr[0:1, :] = jnp.full((1, tq), INT_MIN, I32)
    thr_scr[1:2, :] = jnp.full((1, tq), 2 ** 31 - 1, I32)

    @pl.when((i + 1) * tq > top_k)
    def _():
        t_hi = kth_largest_half(top_k)
        need_lo = top_k - count_half(lambda v: v > t_hi.astype(I16))

        def low_step(j, carry):
            rows = step_rows(j)
            key = key_scr[rows, :]
            low = (key & 0xFFFF) - 2 ** 15
            half_scr[rows, :] = jnp.where((key >> 16) == t_hi, low, -(2 ** 15)).astype(I16)
            return carry
        _pair_loop(0, n_steps, low_step, 0)
        t_lo = kth_largest_half(need_lo)
        thr = (t_hi << 16) | (t_lo + 2 ** 15)
        thr_scr[0:1, :] = thr

        n_ge = count(lambda key, s: key >= thr)

        @pl.when(jnp.max(n_ge) > top_k)
        def _():
            need = top_k - count(lambda key, s: key > thr)

            pos_bits = (seq - 1).bit_length()

            def pos_step(it, j_cur):
                cand = j_cur | jnp.left_shift(jnp.int32(1), pos_bits - 1 - it)
                cnt = count(lambda key, s: (key == thr) & (s < cand))
                return jnp.where(cnt < need, cand, j_cur)
            thr_scr[1:2, :] = lax.fori_loop(0, pos_bits, pos_step, jnp.zeros((1, tq), I32))

    thr = thr_scr[0:1, :]
    j_last = thr_scr[1:2, :]

    def mask_step(j, carry):
        rows = step_rows(j)
        key = key_scr[rows, :]
        s_glob = j * sb + s_iota
        sel = ((key > thr) | ((key == thr) & (s_glob <= j_last))) & (s_glob <= t_glob)
        mask_scr[rows, :] = jnp.where(sel, 0.0, NEG_BIG)
        return carry
    _pair_loop(0, n_steps, mask_step, 0)

    far_steps = jnp.maximum(i - 1, 0)
    diag_tiles = ((0, 1), (2, 0))
    prev_tiles = ((2, 2), (1, 2))

    def logits(g, j, band):
        rows = step_rows(j)
        s = _dot_nt(k_ref[0, rows, g * hd:(g + 1) * hd], qs_scr[g * gw:(g + 1) * gw, :])
        mk = mask_scr[rows, :]
        s = s + jnp.concatenate([mk[:, th * kb:(th + 1) * kb] for th in range(nsub) for _ in range(ATT_GROUP)],
                                axis=1)
        if band:
            s = jnp.concatenate([
                jnp.concatenate([
                    s[sh * kb:(sh + 1) * kb, th * sw:(th + 1) * sw]
                    + bias_ref[g, jnp.where(j == i, diag_tiles[sh][th], prev_tiles[sh][th])]
                    for th in range(nsub)], axis=1)
                for sh in range(nsub)], axis=0)
        s_scr[rows, :] = s
        return fold(s, jnp.max)

    def exp_pv(g, j, m, l8):
        rows = step_rows(j)
        p = jnp.exp2(s_scr[rows, :] - m)
        acc_scr[...] += _dot(vt_ref[0, g * hd:(g + 1) * hd, rows], p.astype(BF16))
        return l8 + fold(p, jnp.sum)

    def finish(g, l8):
        res = acc_scr[...] / jnp.sum(l8, axis=0, keepdims=True)
        for th in range(nsub):
            for hh in range(ATT_GROUP):
                h = g * ATT_GROUP + hh
                out_scr[h * hd:(h + 1) * hd, th * kb:(th + 1) * kb] = res[:, th * sw + hh * kb:th * sw + (hh + 1) * kb]

    neg = jnp.full((8, gw), NEG_BIG, F32)
    zero = jnp.zeros((8, gw), F32)
    m8 = _pair_loop(0, far_steps, lambda j, c: jnp.maximum(c, logits(0, j, False)), neg)
    m8 = lax.fori_loop(far_steps, n_steps, lambda j, c: jnp.maximum(c, logits(0, j, True)), m8)
    for g in range(ATT_KV_HEADS):
        m = jnp.max(m8, axis=0, keepdims=True)
        acc_scr[...] = jnp.zeros_like(acc_scr)
        if g + 1 < ATT_KV_HEADS:
            def both(j, c, band):
                l8, m8n = c
                l8 = exp_pv(g, j, m, l8)
                return l8, jnp.maximum(m8n, logits(g + 1, j, band))
            l8, m8 = _pair_loop(0, far_steps, functools.partial(both, band=False), (zero, neg))
            l8, m8 = lax.fori_loop(far_steps, n_steps, functools.partial(both, band=True), (l8, m8))
        else:
            l8 = _pair_loop(0, n_steps, lambda j, c: exp_pv(g, j, m, c), zero)
        finish(g, l8)

    o_ref[0] = out_scr[...].T.astype(o_ref.dtype)


def _dsa_mixer(h3, gain, w_in, rel_bias):
    B, L, D = h3.shape
    top_k = min(TOPK_MAX, L // 4)
    assert L % KEY_STEP == 0 and top_k % KEY_BLOCK == 0
    tm = 512
    nq = ATT_HEADS * ATT_HEAD_DIM
    nkv = ATT_KV_HEADS * ATT_HEAD_DIM
    nqi = IDX_HEADS * IDX_HEAD_DIM
    width = w_in.shape[1]
    padded = nq + 2 * nkv + nqi + 128
    w = jnp.pad(w_in, ((0, 0), (0, padded - width))).astype(BF16)
    q, k, vt, qi, ki, wt = pl.pallas_call(
        _dsa_proj_kernel,
        grid=(B, L // tm),
        in_specs=[
            pl.BlockSpec((1, tm, D), lambda b, r: (b, r, 0)),
            pl.BlockSpec((1, D), lambda b, r: (0, 0)),
            pl.BlockSpec((D, padded), lambda b, r: (0, 0)),
        ],
        out_specs=[
            pl.BlockSpec((1, tm, nq), lambda b, r: (b, r, 0)),
            pl.BlockSpec((1, tm, nkv), lambda b, r: (b, r, 0)),
            pl.BlockSpec((1, nkv, tm), lambda b, r: (b, 0, r)),
            pl.BlockSpec((1, tm, nqi), lambda b, r: (b, r, 0)),
            pl.BlockSpec((1, tm, IDX_HEAD_DIM), lambda b, r: (b, r, 0)),
            pl.BlockSpec((1, IDX_HEADS, tm), lambda b, r: (b, 0, r)),
        ],
        out_shape=[
            jax.ShapeDtypeStruct((B, L, nq), BF16),
            jax.ShapeDtypeStruct((B, L, nkv), BF16),
            jax.ShapeDtypeStruct((B, nkv, L), BF16),
            jax.ShapeDtypeStruct((B, L, nqi), BF16),
            jax.ShapeDtypeStruct((B, L, IDX_HEAD_DIM), BF16),
            jax.ShapeDtypeStruct((B, IDX_HEADS, L), F32),
        ],
        compiler_params=_params(2),
        name="dsa_proj",
    )(h3, gain.reshape(1, D), w)

    bias = pl.pallas_call(
        _bias_kernel,
        in_specs=[pl.BlockSpec(memory_space=pltpu.SMEM)],
        out_specs=pl.BlockSpec(memory_space=pltpu.VMEM),
        out_shape=jax.ShapeDtypeStruct((ATT_KV_HEADS, BIAS_TILES, KEY_BLOCK, ATT_GROUP * KEY_BLOCK), F32),
        name="dsa_bias_tiles",
    )(rel_bias)

    tq = Q_BLOCK
    return pl.pallas_call(
        functools.partial(_dsa_attn_kernel, top_k=top_k, seq=L),
        grid=(B, L // tq),
        in_specs=[
            pl.BlockSpec((1, tq, nq), lambda b, i: (b, i, 0)),
            pl.BlockSpec((1, L, nkv), lambda b, i: (b, 0, 0)),
            pl.BlockSpec((1, nkv, L), lambda b, i: (b, 0, 0)),
            pl.BlockSpec((1, tq, nqi), lambda b, i: (b, i, 0)),
            pl.BlockSpec((1, L, IDX_HEAD_DIM), lambda b, i: (b, 0, 0)),
            pl.BlockSpec((1, IDX_HEADS, tq), lambda b, i: (b, 0, i)),
            pl.BlockSpec((ATT_KV_HEADS, BIAS_TILES, KEY_BLOCK, ATT_GROUP * KEY_BLOCK), lambda b, i: (0, 0, 0, 0)),
        ],
        out_specs=pl.BlockSpec((1, tq, D), lambda b, i: (b, i, 0)),
        out_shape=jax.ShapeDtypeStruct((B, L, D), BF16),
        scratch_shapes=[
            pltpu.VMEM((ATT_HEADS * tq, ATT_HEAD_DIM), BF16),
            pltpu.VMEM((IDX_HEADS * tq, IDX_HEAD_DIM), BF16),
            pltpu.VMEM((L, tq), I32),
            pltpu.VMEM((L, tq), I16),
            pltpu.VMEM((L, tq), F32),
            pltpu.VMEM((L, ATT_GROUP * tq), F32),
            pltpu.VMEM((ATT_HEAD_DIM, ATT_GROUP * tq), F32),
            pltpu.VMEM((D, tq), F32),
            pltpu.VMEM((8, tq), I32),
        ],
        compiler_params=_params(2),
        name="dsa_attention",
    )(q, k, vt, qi, ki, wt, bias)


def _ffn_kernel(h_ref, o_ref, wo_ref, gain_ref, wup_ref, cw_ref, cb_ref, wdn_ref, fgain_ref, out_ref,
                xn_scr, hbuf, carry_scr, acc_scr, act_scr, *, tiles_per_seq, final):
    tm = h_ref.shape[0]
    d_ff = wdn_ref.shape[0]
    tn = FFN_TILE
    n_ff = d_ff // tn
    i = pl.program_id(0)

    hnew = h_ref[...] + _dot(o_ref[...], wo_ref[...])
    acc_scr[...] = hnew
    xn_scr[...] = _rms_norm(hnew, gain_ref[...]).astype(BF16)
    first = (i % tiles_per_seq) == 0

    @pl.when(i == 0)
    def _():
        carry_scr[...] = jnp.zeros_like(carry_scr)

    def halves(j):
        return ((0, j * tn), (tn, d_ff + j * tn))

    def up(j, slot):
        for l0, c0 in halves(j):
            hu = _dot(xn_scr[...], wup_ref[:, c0:c0 + tn])
            hbuf[slot, 0:8, l0:l0 + tn] = jnp.where(first, 0.0, carry_scr[:, c0:c0 + tn])
            hbuf[slot, 8:8 + tm, l0:l0 + tn] = hu
            carry_scr[:, c0:c0 + tn] = hu[tm - 8:tm, :]

    def down(j, slot):
        ys = []
        for l0, c0 in halves(j):
            cw = cw_ref[:, c0:c0 + tn]
            ys.append(cw[2:3, :] * hbuf[slot, 8:8 + tm, l0:l0 + tn] + cw[1:2, :] * hbuf[slot, 7:7 + tm, l0:l0 + tn]
                      + cw[0:1, :] * hbuf[slot, 6:6 + tm, l0:l0 + tn] + cb_ref[:, c0:c0 + tn])
        gate, upv = ys
        act_scr[:, j * tn:(j + 1) * tn] = (gate * jax.nn.sigmoid(gate) * upv).astype(BF16)

    up(0, 0)
    for j in range(n_ff):
        if j + 1 < n_ff:
            up(j + 1, (j + 1) % 2)
        down(j, j % 2)

    res = acc_scr[...] + _dot(act_scr[...], wdn_ref[...])
    if final:
        res = _rms_norm(res, fgain_ref[...])
    out_ref[...] = res


def _ffn(h2, o2, w_out, gain, w_up, conv_w, conv_b, w_down, final_gain, *, seq, final):
    M, D = h2.shape
    F = w_down.shape[0]
    tm = FFN_ROWS
    tn = FFN_TILE
    assert F % tn == 0 and seq % tm == 0
    const2 = lambda i: (0, 0)
    resident = dict(pipeline_mode=pl.Buffered(1))
    return pl.pallas_call(
        functools.partial(_ffn_kernel, tiles_per_seq=seq // tm, final=final),
        grid=(M // tm,),
        in_specs=[
            pl.BlockSpec((tm, D), lambda i: (i, 0)),
            pl.BlockSpec((tm, D), lambda i: (i, 0)),
            pl.BlockSpec((D, D), const2, **resident),
            pl.BlockSpec((1, D), const2),
            pl.BlockSpec((D, 2 * F), const2, **resident),
            pl.BlockSpec((3, 2 * F), const2),
            pl.BlockSpec((1, 2 * F), const2),
            pl.BlockSpec((F, D), const2, **resident),
            pl.BlockSpec((1, D), const2),
        ],
        out_specs=pl.BlockSpec((tm, D), lambda i: (i, 0)),
        out_shape=jax.ShapeDtypeStruct((M, D), F32),
        scratch_shapes=[
            pltpu.VMEM((tm, D), BF16),
            pltpu.VMEM((2, tm + 8, 2 * tn), F32),
            pltpu.VMEM((8, 2 * F), F32),
            pltpu.VMEM((tm, D), F32),
            pltpu.VMEM((tm, F), BF16),
        ],
        compiler_params=_params(1),
        name="outproj_convffn",
    )(h2, o2, w_out.astype(BF16), gain.reshape(1, D), w_up.astype(BF16), conv_w, conv_b.reshape(1, 2 * F),
      w_down.astype(BF16), final_gain.reshape(1, D))


def kernel(x, attn_norm, ffn_norm, hgrn_w_in, hgrn_w_out, hgrn_gate_norm, hgrn_lower_bounds, dsa_w_in, dsa_w_out, rel_bias, ffn_w_up, ffn_conv_w, ffn_conv_b, ffn_w_down, final_norm):
    B, L, D = x.shape
    depth = attn_norm.shape[0]
    h = x.reshape(B * L, D)
    for layer in range(depth):
        j = layer // 2
        h3 = h.reshape(B, L, D)
        if layer % 2 == 0:
            o = _hgrn_mixer(h3, attn_norm[layer], hgrn_w_in[j], hgrn_lower_bounds, hgrn_gate_norm[j], j)
            w_out = hgrn_w_out[j]
        else:
            o = _dsa_mixer(h3, attn_norm[layer], dsa_w_in[j], rel_bias)
            w_out = dsa_w_out[j]
        h = _ffn(h, o.reshape(B * L, D), w_out, ffn_norm[layer], ffn_w_up[layer], ffn_conv_w[layer],
                 ffn_conv_b[layer], ffn_w_down[layer], final_norm, seq=L, final=(layer == depth - 1))
    return h.reshape(B, L, D)
```

```python
import functools
import math

import jax
import jax.numpy as jnp
from jax import lax
from jax.experimental import pallas as pl
from jax.experimental.pallas import tpu as pltpu

F32 = jnp.float32
BF16 = jnp.bfloat16
I32 = jnp.int32
I16 = jnp.int16

EPS = 1e-6
NEG_BIG = -1e30
TINY = 1e-30

HGRN_HEAD_DIM = 128
HGRN_BLOCK = 256
HGRN_HEADS_PER_STEP = 4
ATT_HEADS = 16
ATT_HEAD_DIM = 64
ATT_KV_HEADS = 2
ATT_GROUP = ATT_HEADS // ATT_KV_HEADS
ATT_V_ROWS = ATT_HEAD_DIM + 16
IDX_HEADS = 8
IDX_HEAD_DIM = 64
TOPK_MAX = 256
REL_BUCKETS = 32
REL_MAX_DIST = 128
KEY_BLOCK = 128
Q_BLOCK = 2 * KEY_BLOCK
KEY_STEP = Q_BLOCK
BIAS_TILES = 3
FFN_ROWS = 512
FFN_TILE = 256
INT_MIN = -(2 ** 31)
LOG2E = 1.4426950408889634

VMEM_LIMIT_BYTES = 56 * 1024 * 1024


def _dot(a, b):
    return jnp.dot(a, b, preferred_element_type=F32)


def _dot_nt(a, b):
    return lax.dot_general(a, b, (((1,), (1,)), ((), ())), preferred_element_type=F32)


def _rms_norm(x, gain):
    y = x * lax.rsqrt(jnp.mean(x * x, axis=-1, keepdims=True) + EPS)
    return y * gain


def _pair_loop(lo, hi, body, init):
    def two(p, c):
        j = lo + 2 * p
        return body(j + 1, body(j, c))
    c = lax.fori_loop(0, (hi - lo) // 2, two, init)
    return lax.cond((hi - lo) % 2 == 1, lambda c: body(hi - 1, c), lambda c: c, c)


def _params(n_axes):
    return pltpu.CompilerParams(dimension_semantics=("arbitrary",) * n_axes,
                                vmem_limit_bytes=VMEM_LIMIT_BYTES)


def _block_rows(a, m):
    c = a.shape[0]
    return a.reshape(c // m, m, a.shape[1])[:, m - 1:m, :]


def _spread_rows(r, m):
    n = r.shape[0]
    return jnp.broadcast_to(r, (n, m, r.shape[2])).reshape(n * m, r.shape[2])


def _hgrn_kernel(x_ref, gain_ref, wq_ref, wf_ref, wi_ref, wg_ref, lbraw_ref, gn_ref, o_ref,
                 xn_scr, proj_scr, st_scr, *, layer_j, seq):
    C = HGRN_BLOCK
    dh = HGRN_HEAD_DIM
    pw = HGRN_HEADS_PER_STEP * dh
    head = pl.program_id(1)

    @pl.when(head == 0)
    def _():
        rb = 256

        def norm_rows(r, carry):
            rows = pl.ds(pl.multiple_of(r * rb, rb), rb)
            xn_scr[rows, :] = _rms_norm(x_ref[0, rows, :], gain_ref[...]).astype(BF16)
            return carry
        lax.fori_loop(0, seq // rb, norm_rows, 0)

    pb = 512

    def proj_rows(r, carry):
        rows = pl.ds(pl.multiple_of(r * pb, pb), pb)
        xr = xn_scr[rows, :]
        for part, w_ref in enumerate((wq_ref, wf_ref, wi_ref, wg_ref)):
            proj_scr[rows, part * pw:(part + 1) * pw] = _dot(xr, w_ref[...])
        return carry
    lax.fori_loop(0, seq // pb, proj_rows, 0)

    raw = lbraw_ref[...]
    ex = jnp.exp(raw - jnp.max(raw, axis=0, keepdims=True))
    soft = ex / jnp.sum(ex, axis=0, keepdims=True)
    lb_all = jnp.zeros((1, raw.shape[1]), F32)
    for l in range(1, layer_j + 1):
        lb_all = lb_all + soft[l:l + 1, :]
    gn = gn_ref[...]

    ri = lax.broadcasted_iota(I32, (C, C), 0)
    ci = lax.broadcasted_iota(I32, (C, C), 1)
    tril = (ci <= ri).astype(BF16)
    trow = lax.broadcasted_iota(I32, (C, dh), 0)
    sub = trow & 7

    def same_block(n):
        sh = int(math.log2(n))
        return ((ri >> sh) == (ci >> sh)).astype(F32)

    st_scr[...] = jnp.zeros_like(st_scr)

    def head_step(rows, hh):
        c0 = hh * dh
        q = proj_scr[rows, c0:c0 + dh]
        f = proj_scr[rows, pw + c0:pw + c0 + dh]
        iv = proj_scr[rows, 2 * pw + c0:2 * pw + c0 + dh]
        g = proj_scr[rows, 3 * pw + c0:3 * pw + c0 + dh]
        lb = lb_all[:, hh * dh:(hh + 1) * dh]
        oml = 1.0 - lb

        e = jnp.exp(-jnp.abs(f))
        r = 1.0 / (1.0 + e)
        er = e * r
        pos = f >= 0.0
        forget = lb + oml * jnp.where(pos, r, er)
        logf = jnp.log(jnp.maximum(forget, TINY)) * LOG2E
        k = oml * jnp.where(pos, er, r)

        hi = logf.astype(BF16)
        r1 = logf - hi.astype(F32)
        mid = r1.astype(BF16)
        lo = (r1 - mid.astype(F32)).astype(BF16)
        b = _dot(tril, hi) + _dot(tril, mid) + _dot(tril, lo)

        b8 = b.reshape(C // 8, 8, dh)
        qs, ks = [], []
        for rho in range(8):
            bref = _spread_rows(b8[:, rho:rho + 1, :], 8)
            qs.append(jnp.where(sub >= rho, q * jnp.exp2(b - bref), 0.0).astype(BF16))
            ks.append(jnp.where(sub == rho, k, 0.0).astype(BF16))
        acc = _dot_nt(jnp.concatenate(qs, axis=1), jnp.concatenate(ks, axis=1))
        for m in [8 << lv for lv in range(int(math.log2(C // 8)))]:
            last = _block_rows(b, m)
            prev = jnp.concatenate([jnp.zeros((1, 1, dh), F32), last[:-1]], axis=0)
            upper = ((trow >> int(math.log2(m))) & 1) == 1
            qm = jnp.where(upper, q * jnp.exp2(b - _spread_rows(prev, m)), 0.0).astype(BF16)
            km = jnp.where(upper, 0.0, k * jnp.exp2(_spread_rows(last, m) - b)).astype(BF16)
            acc = _dot_nt(qm, km) + same_block(m) * acc

        st = st_scr[hh]
        o = _dot(acc.astype(BF16), iv.astype(BF16)) + _dot_nt((q * jnp.exp2(b)).astype(BF16), st.astype(BF16))
        b_last = b[C - 1:C, :]
        kh = (k * jnp.exp2(b_last - b)).astype(BF16)
        st_scr[hh] = st * jnp.exp2(b_last) + _dot(iv.T.astype(BF16), kh)

        o = o * lax.rsqrt(jnp.mean(o * o, axis=-1, keepdims=True) + EPS) * gn
        o = o * (g * jax.nn.sigmoid(g))
        o_ref[0, rows, hh * dh:(hh + 1) * dh] = o.astype(o_ref.dtype)

    def step(c, carry):
        rows = pl.ds(pl.multiple_of(c * C, C), C)
        for hh in range(HGRN_HEADS_PER_STEP):
            head_step(rows, hh)
        return carry

    lax.fori_loop(0, seq // C, step, 0)


def _hgrn_mixer(h3, gain, w_in, lower_raw, gate_norm, layer_j):
    B, L, D = h3.shape
    dh = HGRN_HEAD_DIM
    H = D // dh
    n_layers = lower_raw.shape[0]
    hp = HGRN_HEADS_PER_STEP
    assert H % hp == 0
    steps = H // hp
    w = w_in.astype(BF16)

    def part_spec(part):
        return pl.BlockSpec((D, hp * dh), lambda b, h: (0, part * steps + h))
    return pl.pallas_call(
        functools.partial(_hgrn_kernel, layer_j=layer_j, seq=L),
        grid=(B, steps),
        in_specs=[
            pl.BlockSpec((1, L, D), lambda b, h: (b, 0, 0)),
            pl.BlockSpec((1, D), lambda b, h: (0, 0)),
            part_spec(0), part_spec(1), part_spec(2), part_spec(3),
            pl.BlockSpec((n_layers, hp * dh), lambda b, h: (0, h)),
            pl.BlockSpec((1, dh), lambda b, h: (0, 0)),
        ],
        out_specs=pl.BlockSpec((1, L, hp * dh), lambda b, h: (b, 0, h)),
        out_shape=jax.ShapeDtypeStruct((B, L, D), BF16),
        scratch_shapes=[pltpu.VMEM((L, D), BF16), pltpu.VMEM((L, hp * 4 * dh), F32),
                        pltpu.VMEM((hp, dh, dh), F32)],
        compiler_params=_params(2),
        name="hgrn_mixer",
    )(h3, gain.reshape(1, D), w, w, w, w, lower_raw, gate_norm.reshape(1, dh))


def _dsa_proj_kernel(x_ref, gain_ref, w_ref, q_ref, k_ref, vt_ref, qi_ref, ki_ref, wt_ref):
    nq = ATT_HEADS * ATT_HEAD_DIM
    nkv = ATT_KV_HEADS * ATT_HEAD_DIM
    nqi = IDX_HEADS * IDX_HEAD_DIM
    xn = _rms_norm(x_ref[0], gain_ref[...]).astype(BF16)
    p = _dot(xn, w_ref[...])
    c0 = 0
    q_ref[0] = (p[:, c0:c0 + nq] * (ATT_HEAD_DIM ** -0.5 * LOG2E)).astype(BF16)
    c0 += nq
    k_ref[0] = p[:, c0:c0 + nkv].astype(BF16)
    c0 += nkv
    vt = p[:, c0:c0 + nkv].T
    tail = (lax.broadcasted_iota(I32, (ATT_V_ROWS - ATT_HEAD_DIM, vt.shape[1]), 0) == 0).astype(F32)
    vt_ref[0] = jnp.concatenate(
        [blk for g in range(ATT_KV_HEADS) for blk in (vt[g * ATT_HEAD_DIM:(g + 1) * ATT_HEAD_DIM, :], tail)],
        axis=0).astype(BF16)
    c0 += nkv
    qi_ref[0] = p[:, c0:c0 + nqi].astype(BF16)
    c0 += nqi
    ki_ref[0] = p[:, c0:c0 + IDX_HEAD_DIM].astype(BF16)
    tail = p[:, c0:c0 + 128].T
    wt_ref[0] = tail[IDX_HEAD_DIM:IDX_HEAD_DIM + IDX_HEADS, :] * (IDX_HEADS ** -0.5 * IDX_HEAD_DIM ** -0.5)


def _bias_kernel(rb_ref, out_ref):
    kb = KEY_BLOCK
    s = lax.broadcasted_iota(I32, (kb, kb), 0)
    t = lax.broadcasted_iota(I32, (kb, kb), 1)
    max_exact = REL_BUCKETS // 2
    for tile in range(2):
        n = jnp.maximum(t - s + kb * tile, 0)
        nf = jnp.maximum(n, max_exact).astype(F32)
        large = max_exact + (jnp.log(nf / max_exact) / math.log(REL_MAX_DIST / max_exact)
                             * (REL_BUCKETS - max_exact)).astype(I32)
        large = jnp.minimum(large, REL_BUCKETS - 1)
        bucket = jnp.where(n < max_exact, n, large)
        for h in range(ATT_HEADS):
            def pick(bk, acc):
                return jnp.where(bucket == bk, rb_ref[bk, h], acc)
            tile_h = lax.fori_loop(0, REL_BUCKETS, pick, jnp.zeros((kb, kb), F32))
            out_ref[h // ATT_GROUP, tile, :, (h % ATT_GROUP) * kb:(h % ATT_GROUP + 1) * kb] = (
                (tile_h - rb_ref[REL_BUCKETS - 1, h]) * LOG2E)
    for g in range(ATT_KV_HEADS):
        out_ref[g, 2] = jnp.zeros((kb, ATT_GROUP * kb), F32)


def _dsa_attn_kernel(q_ref, k_ref, vt_ref, qi_ref, ki_ref, wt_ref, bias_ref, o_ref,
                     qs_scr, qis_scr, key_scr, half_scr, mask_scr, s0_scr, s1_scr, acc_scr, out_scr, thr_scr,
                     *, top_k, seq):
    tq = Q_BLOCK
    kb = KEY_BLOCK
    sb = KEY_STEP
    nsub = tq // kb
    hd = ATT_HEAD_DIM
    sw = ATT_GROUP * kb
    gw = nsub * sw
    i = pl.program_id(1)
    n_steps = i + 1

    for g in range(ATT_KV_HEADS):
        for th in range(nsub):
            for hh in range(ATT_GROUP):
                r0 = g * gw + th * sw + hh * kb
                c0 = (g * ATT_GROUP + hh) * hd
                qs_scr[r0:r0 + kb, :] = q_ref[0, th * kb:(th + 1) * kb, c0:c0 + hd]
    for h in range(IDX_HEADS):
        qis_scr[h * tq:(h + 1) * tq, :] = qi_ref[0, :, h * IDX_HEAD_DIM:(h + 1) * IDX_HEAD_DIM]
    wt = wt_ref[0]

    s_iota = lax.broadcasted_iota(I32, (sb, tq), 0)
    t_glob = i * tq + lax.broadcasted_iota(I32, (sb, tq), 1)

    def step_rows(j):
        return pl.ds(pl.multiple_of(j * sb, sb), sb)

    def fold(x, op):
        return op(x.reshape(sb // 8, 8, x.shape[1]), axis=0)

    def idx_step(j, carry):
        rows = step_rows(j)
        d = _dot_nt(ki_ref[0, rows, :], qis_scr[...])
        sc = jnp.zeros((sb, tq), F32)
        for h in range(IDX_HEADS):
            sc = sc + wt[h:h + 1, :] * jnp.maximum(d[:, h * tq:(h + 1) * tq], 0.0)
        sc = jnp.where(j * sb + s_iota > t_glob, NEG_BIG, sc) + 0.0
        bits = pltpu.bitcast(sc, I32)
        key = jnp.where(bits < 0, bits ^ 0x7FFFFFFF, bits)
        key_scr[rows, :] = key
        half_scr[rows, :] = (key >> 16).astype(I16)
        return carry
    _pair_loop(0, n_steps, idx_step, 0)

    def count(pred):
        def body(j, cnt):
            hit = pred(key_scr[step_rows(j), :], j * sb + s_iota)
            return cnt + fold(jnp.where(hit, 1, 0), jnp.sum)
        return jnp.sum(_pair_loop(0, n_steps, body, jnp.zeros((8, tq), I32)), axis=0, keepdims=True)

    def count_half(preds):
        one, nil = jnp.ones((), I16), jnp.zeros((), I16)

        def body(j, cnts):
            v = half_scr[step_rows(j), :]
            out = []
            for pred, cnt in zip(preds, cnts):
                ones = jnp.where(pred(v), one, nil)
                for r in range(sb // 16):
                    cnt = cnt + ones[r * 16:(r + 1) * 16, :]
                out.append(cnt)
            return tuple(out)
        cnts = _pair_loop(0, n_steps, body, (jnp.zeros((16, tq), I16),) * len(preds))
        return [jnp.sum(c.astype(I32), axis=0, keepdims=True) for c in cnts]

    def at_least(cand):
        c16 = cand.astype(I16)
        return lambda v: v >= c16

    def kth_largest_half(target):
        c0, = count_half([at_least(jnp.zeros((1, tq), I32))])
        t0 = jnp.where(c0 >= target, 0, -(2 ** 15)).astype(I32)

        def bit_step(it, t_cur):
            cand = t_cur | jnp.left_shift(jnp.int32(1), 14 - it)
            cnt, = count_half([at_least(cand)])
            return jnp.where(cnt >= target, cand, t_cur)
        return lax.fori_loop(0, 15, bit_step, t0)

    thr_scr[0:1, :] = jnp.full((1, tq), INT_MIN, I32)
    thr_scr[1:2, :] = jnp.full((1, tq), 2 ** 31 - 1, I32)

    @pl.when((i + 1) * tq > top_k)
    def _():
        t_hi = kth_largest_half(top_k)
        n_above, = count_half([lambda v: v > t_hi.astype(I16)])
        need_lo = top_k - n_above

        def low_step(j, carry):
            rows = step_rows(j)
            key = key_scr[rows, :]
            low = (key & 0xFFFF) - 2 ** 15
            half_scr[rows, :] = jnp.where((key >> 16) == t_hi, low, -(2 ** 15)).astype(I16)
            return carry
        _pair_loop(0, n_steps, low_step, 0)
        t_lo = kth_largest_half(need_lo)
        thr = (t_hi << 16) | (t_lo + 2 ** 15)
        thr_scr[0:1, :] = thr

        n_ge = count(lambda key, s: key >= thr)

        @pl.when(jnp.max(n_ge) > top_k)
        def _():
            need = top_k - count(lambda key, s: key > thr)

            pos_bits = (seq - 1).bit_length()

            def pos_step(it, j_cur):
                cand = j_cur | jnp.left_shift(jnp.int32(1), pos_bits - 1 - it)
                cnt = count(lambda key, s: (key == thr) & (s < cand))
                return jnp.where(cnt < need, cand, j_cur)
            thr_scr[1:2, :] = lax.fori_loop(0, pos_bits, pos_step, jnp.zeros((1, tq), I32))

    thr = thr_scr[0:1, :]
    j_last = thr_scr[1:2, :]

    def mask_step(j, carry):
        rows = step_rows(j)
        key = key_scr[rows, :]
        s_glob = j * sb + s_iota
        sel = ((key > thr) | ((key == thr) & (s_glob <= j_last))) & (s_glob <= t_glob)
        mask_scr[rows, :] = jnp.where(sel, 0.0, NEG_BIG)
        return carry
    _pair_loop(0, n_steps, mask_step, 0)

    s_bufs = (s0_scr, s1_scr)
    far_steps = jnp.maximum(i - 1, 0)
    diag_tiles = ((0, 1), (2, 0))
    prev_tiles = ((2, 2), (1, 2))

    def logits(g, j, band):
        rows = step_rows(j)
        s = _dot_nt(k_ref[0, rows, g * hd:(g + 1) * hd], qs_scr[g * gw:(g + 1) * gw, :])
        mk = mask_scr[rows, :]
        s = s + jnp.concatenate([mk[:, th * kb:(th + 1) * kb] for th in range(nsub) for _ in range(ATT_GROUP)],
                                axis=1)
        if band:
            s = jnp.concatenate([
                jnp.concatenate([
                    s[sh * kb:(sh + 1) * kb, th * sw:(th + 1) * sw]
                    + bias_ref[g, jnp.where(j == i, diag_tiles[sh][th], prev_tiles[sh][th])]
                    for th in range(nsub)], axis=1)
                for sh in range(nsub)], axis=0)
        s_bufs[g % 2][rows, :] = s
        return fold(s, jnp.max)

    def exp_pv(g, j, m, carry):
        rows = step_rows(j)
        p = jnp.exp2(s_bufs[g % 2][rows, :] - m)
        acc_scr[...] += _dot(vt_ref[0, g * ATT_V_ROWS:(g + 1) * ATT_V_ROWS, rows], p.astype(BF16))
        return carry

    def finish(g):
        res = acc_scr[0:hd, :] / acc_scr[hd:hd + 1, :]
        for th in range(nsub):
            for hh in range(ATT_GROUP):
                h = g * ATT_GROUP + hh
                out_scr[h * hd:(h + 1) * hd, th * kb:(th + 1) * kb] = res[:, th * sw + hh * kb:th * sw + (hh + 1) * kb]

    neg = jnp.full((8, gw), NEG_BIG, F32)
    m8 = _pair_loop(0, far_steps, lambda j, c: jnp.maximum(c, logits(0, j, False)), neg)
    m8 = lax.fori_loop(far_steps, n_steps, lambda j, c: jnp.maximum(c, logits(0, j, True)), m8)
    for g in range(ATT_KV_HEADS):
        m = jnp.max(m8, axis=0, keepdims=True)
        acc_scr[...] = jnp.zeros_like(acc_scr)
        if g + 1 < ATT_KV_HEADS:
            def both(j, m8n, band):
                exp_pv(g, j, m, 0)
                return jnp.maximum(m8n, logits(g + 1, j, band))
            m8 = _pair_loop(0, far_steps, functools.partial(both, band=False), neg)
            m8 = lax.fori_loop(far_steps, n_steps, functools.partial(both, band=True), m8)
        else:
            _pair_loop(0, n_steps, lambda j, c: exp_pv(g, j, m, c), 0)
        finish(g)

    o_ref[0] = out_scr[...].T.astype(o_ref.dtype)


def _dsa_mixer(h3, gain, w_in, rel_bias):
    B, L, D = h3.shape
    top_k = min(TOPK_MAX, L // 4)
    assert L % KEY_STEP == 0 and top_k % KEY_BLOCK == 0
    tm = 512
    nq = ATT_HEADS * ATT_HEAD_DIM
    nkv = ATT_KV_HEADS * ATT_HEAD_DIM
    nqi = IDX_HEADS * IDX_HEAD_DIM
    nvt = ATT_KV_HEADS * ATT_V_ROWS
    width = w_in.shape[1]
    padded = nq + 2 * nkv + nqi + 128
    w = jnp.pad(w_in, ((0, 0), (0, padded - width))).astype(BF16)
    q, k, vt, qi, ki, wt = pl.pallas_call(
        _dsa_proj_kernel,
        grid=(B, L // tm),
        in_specs=[
            pl.BlockSpec((1, tm, D), lambda b, r: (b, r, 0)),
            pl.BlockSpec((1, D), lambda b, r: (0, 0)),
            pl.BlockSpec((D, padded), lambda b, r: (0, 0)),
        ],
        out_specs=[
            pl.BlockSpec((1, tm, nq), lambda b, r: (b, r, 0)),
            pl.BlockSpec((1, tm, nkv), lambda b, r: (b, r, 0)),
            pl.BlockSpec((1, nvt, tm), lambda b, r: (b, 0, r)),
            pl.BlockSpec((1, tm, nqi), lambda b, r: (b, r, 0)),
            pl.BlockSpec((1, tm, IDX_HEAD_DIM), lambda b, r: (b, r, 0)),
            pl.BlockSpec((1, IDX_HEADS, tm), lambda b, r: (b, 0, r)),
        ],
        out_shape=[
            jax.ShapeDtypeStruct((B, L, nq), BF16),
            jax.ShapeDtypeStruct((B, L, nkv), BF16),
            jax.ShapeDtypeStruct((B, nvt, L), BF16),
            jax.ShapeDtypeStruct((B, L, nqi), BF16),
            jax.ShapeDtypeStruct((B, L, IDX_HEAD_DIM), BF16),
            jax.ShapeDtypeStruct((B, IDX_HEADS, L), F32),
        ],
        compiler_params=_params(2),
        name="dsa_proj",
    )(h3, gain.reshape(1, D), w)

    bias = pl.pallas_call(
        _bias_kernel,
        in_specs=[pl.BlockSpec(memory_space=pltpu.SMEM)],
        out_specs=pl.BlockSpec(memory_space=pltpu.VMEM),
        out_shape=jax.ShapeDtypeStruct((ATT_KV_HEADS, BIAS_TILES, KEY_BLOCK, ATT_GROUP * KEY_BLOCK), F32),
        name="dsa_bias_tiles",
    )(rel_bias)

    tq = Q_BLOCK
    return pl.pallas_call(
        functools.partial(_dsa_attn_kernel, top_k=top_k, seq=L),
        grid=(B, L // tq),
        in_specs=[
            pl.BlockSpec((1, tq, nq), lambda b, i: (b, i, 0)),
            pl.BlockSpec((1, L, nkv), lambda b, i: (b, 0, 0)),
            pl.BlockSpec((1, nvt, L), lambda b, i: (b, 0, 0)),
            pl.BlockSpec((1, tq, nqi), lambda b, i: (b, i, 0)),
            pl.BlockSpec((1, L, IDX_HEAD_DIM), lambda b, i: (b, 0, 0)),
            pl.BlockSpec((1, IDX_HEADS, tq), lambda b, i: (b, 0, i)),
            pl.BlockSpec((ATT_KV_HEADS, BIAS_TILES, KEY_BLOCK, ATT_GROUP * KEY_BLOCK), lambda b, i: (0, 0, 0, 0)),
        ],
        out_specs=pl.BlockSpec((1, tq, D), lambda b, i: (b, i, 0)),
        out_shape=jax.ShapeDtypeStruct((B, L, D), BF16),
        scratch_shapes=[
            pltpu.VMEM((ATT_HEADS * tq, ATT_HEAD_DIM), BF16),
            pltpu.VMEM((IDX_HEADS * tq, IDX_HEAD_DIM), BF16),
            pltpu.VMEM((L, tq), I32),
            pltpu.VMEM((L, tq), I16),
            pltpu.VMEM((L, tq), F32),
            pltpu.VMEM((L, ATT_GROUP * tq), F32),
            pltpu.VMEM((L, ATT_GROUP * tq), F32),
            pltpu.VMEM((ATT_V_ROWS, ATT_GROUP * tq), F32),
            pltpu.VMEM((D, tq), F32),
            pltpu.VMEM((8, tq), I32),
        ],
        compiler_params=_params(2),
        name="dsa_attention",
    )(q, k, vt, qi, ki, wt, bias)


def _ffn_kernel(h_ref, o_ref, wo_ref, gain_ref, wup_ref, cw_ref, cb_ref, wdn_ref, fgain_ref, out_ref,
                xn_scr, hbuf, carry_scr, acc_scr, act_scr, *, tiles_per_seq, final):
    tm = h_ref.shape[0]
    d_ff = wdn_ref.shape[0]
    tn = FFN_TILE
    n_ff = d_ff // tn
    i = pl.program_id(0)

    hnew = h_ref[...] + _dot(o_ref[...], wo_ref[...])
    acc_scr[...] = hnew
    xn_scr[...] = _rms_norm(hnew, gain_ref[...]).astype(BF16)
    first = (i % tiles_per_seq) == 0

    @pl.when(i == 0)
    def _():
        carry_scr[...] = jnp.zeros_like(carry_scr)

    def halves(j):
        return ((0, j * tn), (tn, d_ff + j * tn))

    def up(j, slot):
        for l0, c0 in halves(j):
            hu = _dot(xn_scr[...], wup_ref[:, c0:c0 + tn])
            hbuf[slot, 0:8, l0:l0 + tn] = jnp.where(first, 0.0, carry_scr[:, c0:c0 + tn])
            hbuf[slot, 8:8 + tm, l0:l0 + tn] = hu
            carry_scr[:, c0:c0 + tn] = hu[tm - 8:tm, :]

    def down(j, slot):
        ys = []
        for l0, c0 in halves(j):
            cw = cw_ref[:, c0:c0 + tn]
            ys.append(cw[2:3, :] * hbuf[slot, 8:8 + tm, l0:l0 + tn] + cw[1:2, :] * hbuf[slot, 7:7 + tm, l0:l0 + tn]
                      + cw[0:1, :] * hbuf[slot, 6:6 + tm, l0:l0 + tn] + cb_ref[:, c0:c0 + tn])
        gate, upv = ys
        act_scr[:, j * tn:(j + 1) * tn] = (gate * jax.nn.sigmoid(gate) * upv).astype(BF16)

    up(0, 0)
    for j in range(n_ff):
        if j + 1 < n_ff:
            up(j + 1, (j + 1) % 2)
        down(j, j % 2)

    res = acc_scr[...] + _dot(act_scr[...], wdn_ref[...])
    if final:
        res = _rms_norm(res, fgain_ref[...])
    out_ref[...] = res


def _ffn(h2, o2, w_out, gain, w_up, conv_w, conv_b, w_down, final_gain, *, seq, final):
    M, D = h2.shape
    F = w_down.shape[0]
    tm = FFN_ROWS
    tn = FFN_TILE
    assert F % tn == 0 and seq % tm == 0
    const2 = lambda i: (0, 0)
    resident = dict(pipeline_mode=pl.Buffered(1))
    return pl.pallas_call(
        functools.partial(_ffn_kernel, tiles_per_seq=seq // tm, final=final),
        grid=(M // tm,),
        in_specs=[
            pl.BlockSpec((tm, D), lambda i: (i, 0)),
            pl.BlockSpec((tm, D), lambda i: (i, 0)),
            pl.BlockSpec((D, D), const2, **resident),
            pl.BlockSpec((1, D), const2),
            pl.BlockSpec((D, 2 * F), const2, **resident),
            pl.BlockSpec((3, 2 * F), const2),
            pl.BlockSpec((1, 2 * F), const2),
            pl.BlockSpec((F, D), const2, **resident),
            pl.BlockSpec((1, D), const2),
        ],
        out_specs=pl.BlockSpec((tm, D), lambda i: (i, 0)),
        out_shape=jax.ShapeDtypeStruct((M, D), F32),
        scratch_shapes=[
            pltpu.VMEM((tm, D), BF16),
            pltpu.VMEM((2, tm + 8, 2 * tn), F32),
            pltpu.VMEM((8, 2 * F), F32),
            pltpu.VMEM((tm, D), F32),
            pltpu.VMEM((tm, F), BF16),
        ],
        compiler_params=_params(1),
        name="outproj_convffn",
    )(h2, o2, w_out.astype(BF16), gain.reshape(1, D), w_up.astype(BF16), conv_w, conv_b.reshape(1, 2 * F),
      w_down.astype(BF16), final_gain.reshape(1, D))


def kernel(x, attn_norm, ffn_norm, hgrn_w_in, hgrn_w_out, hgrn_gate_norm, hgrn_lower_bounds, dsa_w_in, dsa_w_out, rel_bias, ffn_w_up, ffn_conv_w, ffn_conv_b, ffn_w_down, final_norm):
    B, L, D = x.shape
    depth = attn_norm.shape[0]
    h = x.reshape(B * L, D)
    for layer in range(depth):
        j = layer // 2
        h3 = h.reshape(B, L, D)
        if layer % 2 == 0:
            o = _hgrn_mixer(h3, attn_norm[layer], hgrn_w_in[j], hgrn_lower_bounds, hgrn_gate_norm[j], j)
            w_out = hgrn_w_out[j]
        else:
            o = _dsa_mixer(h3, attn_norm[layer], dsa_w_in[j], rel_bias)
            w_out = dsa_w_out[j]
        h = _ffn(h, o.reshape(B * L, D), w_out, ffn_norm[layer], ffn_w_up[layer], ffn_conv_w[layer],
                 ffn_conv_b[layer], ffn_w_down[layer], final_norm, seq=L, final=(layer == depth - 1))
    return h.reshape(B, L, D)
```

```python
import functools
import math

import jax
import jax.numpy as jnp
from jax import lax
from jax.experimental import pallas as pl
from jax.experimental.pallas import tpu as pltpu

F32 = jnp.float32
BF16 = jnp.bfloat16
I32 = jnp.int32
I16 = jnp.int16

EPS = 1e-6
NEG_BIG = -1e30
TINY = 1e-30

HGRN_HEAD_DIM = 128
HGRN_BLOCK = 256
HGRN_SUB = HGRN_BLOCK
HGRN_HEADS_PER_STEP = 4
ATT_HEADS = 16
ATT_HEAD_DIM = 64
ATT_KV_HEADS = 2
ATT_GROUP = ATT_HEADS // ATT_KV_HEADS
ATT_V_ROWS = ATT_HEAD_DIM + 16
IDX_HEADS = 8
IDX_HEAD_DIM = 64
TOPK_MAX = 256
REL_BUCKETS = 32
REL_MAX_DIST = 128
KEY_BLOCK = 128
Q_BLOCK = 2 * KEY_BLOCK
KEY_STEP = Q_BLOCK
BIAS_TILES = 3
FFN_ROWS = 512
FFN_TILE = 256
INT_MIN = -(2 ** 31)
LOG2E = 1.4426950408889634

VMEM_LIMIT_BYTES = 56 * 1024 * 1024


def _dot(a, b):
    return jnp.dot(a, b, preferred_element_type=F32)


def _dot_nt(a, b):
    return lax.dot_general(a, b, (((1,), (1,)), ((), ())), preferred_element_type=F32)


def _rms_norm(x, gain):
    y = x * lax.rsqrt(jnp.mean(x * x, axis=-1, keepdims=True) + EPS)
    return y * gain


def _pair_loop(lo, hi, body, init):
    def two(p, c):
        j = lo + 2 * p
        return body(j + 1, body(j, c))
    c = lax.fori_loop(0, (hi - lo) // 2, two, init)
    return lax.cond((hi - lo) % 2 == 1, lambda c: body(hi - 1, c), lambda c: c, c)


def _quad_loop(lo, hi, body, init):
    n = hi - lo

    def run(start, width, c):
        for d in range(width):
            c = body(start + d, c)
        return c
    c = lax.fori_loop(0, n // 4, lambda p, c: run(lo + 4 * p, 4, c), init)
    done = lo + (n // 4) * 4
    c = lax.cond((n & 2) != 0, lambda c: run(done, 2, c), lambda c: c, c)
    return lax.cond((n & 1) != 0, lambda c: body(hi - 1, c), lambda c: c, c)


def _params(n_axes):
    return pltpu.CompilerParams(dimension_semantics=("arbitrary",) * n_axes,
                                vmem_limit_bytes=VMEM_LIMIT_BYTES)


def _block_rows(a, m):
    c = a.shape[0]
    return a.reshape(c // m, m, a.shape[1])[:, m - 1:m, :]


def _spread_rows(r, m):
    n = r.shape[0]
    return jnp.broadcast_to(r, (n, m, r.shape[2])).reshape(n * m, r.shape[2])


def _hgrn_kernel(x_ref, gain_ref, wq_ref, wf_ref, wi_ref, wg_ref, lbraw_ref, gn_ref, o_ref,
                 xn_scr, proj_scr, st_scr, *, layer_j, seq):
    C = HGRN_BLOCK
    dh = HGRN_HEAD_DIM
    pw = HGRN_HEADS_PER_STEP * dh
    head = pl.program_id(1)

    @pl.when(head == 0)
    def _():
        rb = 256

        def norm_rows(r, carry):
            rows = pl.ds(pl.multiple_of(r * rb, rb), rb)
            xn_scr[rows, :] = _rms_norm(x_ref[0, rows, :], gain_ref[...]).astype(BF16)
            return carry
        lax.fori_loop(0, seq // rb, norm_rows, 0)

    pb = 512

    def proj_rows(r, carry):
        rows = pl.ds(pl.multiple_of(r * pb, pb), pb)
        xr = xn_scr[rows, :]
        for part, w_ref in enumerate((wq_ref, wf_ref, wi_ref, wg_ref)):
            proj_scr[rows, part * pw:(part + 1) * pw] = _dot(xr, w_ref[...])
        return carry
    lax.fori_loop(0, seq // pb, proj_rows, 0)

    raw = lbraw_ref[...]
    ex = jnp.exp(raw - jnp.max(raw, axis=0, keepdims=True))
    soft = ex / jnp.sum(ex, axis=0, keepdims=True)
    lb_all = jnp.zeros((1, raw.shape[1]), F32)
    for l in range(1, layer_j + 1):
        lb_all = lb_all + soft[l:l + 1, :]
    gn = gn_ref[...]

    S = HGRN_SUB
    tril = (lax.broadcasted_iota(I32, (C, C), 1) <= lax.broadcasted_iota(I32, (C, C), 0)).astype(BF16)
    ri = lax.broadcasted_iota(I32, (S, S), 0)
    ci = lax.broadcasted_iota(I32, (S, S), 1)
    trow = lax.broadcasted_iota(I32, (S, dh), 0)
    sub = trow & 7

    def same_block(n):
        sh = int(math.log2(n))
        return ((ri >> sh) == (ci >> sh)).astype(F32)

    st_scr[...] = jnp.zeros_like(st_scr)

    def head_step(rows, hh):
        c0 = hh * dh
        q = proj_scr[rows, c0:c0 + dh]
        f = proj_scr[rows, pw + c0:pw + c0 + dh]
        iv = proj_scr[rows, 2 * pw + c0:2 * pw + c0 + dh]
        g = proj_scr[rows, 3 * pw + c0:3 * pw + c0 + dh]
        lb = lb_all[:, hh * dh:(hh + 1) * dh]
        oml = 1.0 - lb

        e = jnp.exp(-jnp.abs(f))
        r = 1.0 / (1.0 + e)
        er = e * r
        pos = f >= 0.0
        forget = lb + oml * jnp.where(pos, r, er)
        logf = jnp.log(jnp.maximum(forget, TINY)) * LOG2E
        k = oml * jnp.where(pos, er, r)

        hi = logf.astype(BF16)
        r1 = logf - hi.astype(F32)
        mid = r1.astype(BF16)
        lo = (r1 - mid.astype(F32)).astype(BF16)
        b = _dot(tril, hi) + _dot(tril, mid) + _dot(tril, lo)

        def sub_scores(q, k, b):
            b8 = b.reshape(S // 8, 8, dh)
            qs, ks = [], []
            for rho in range(8):
                bref = _spread_rows(b8[:, rho:rho + 1, :], 8)
                qs.append(jnp.where(sub >= rho, q * jnp.exp2(b - bref), 0.0).astype(BF16))
                ks.append(jnp.where(sub == rho, k, 0.0).astype(BF16))
            acc = _dot_nt(jnp.concatenate(qs, axis=1), jnp.concatenate(ks, axis=1))
            for m in [8 << lv for lv in range(int(math.log2(S // 8)))]:
                last = _block_rows(b, m)
                prev = jnp.concatenate([jnp.zeros((1, 1, dh), F32), last[:-1]], axis=0)
                upper = ((trow >> int(math.log2(m))) & 1) == 1
                qm = jnp.where(upper, q * jnp.exp2(b - _spread_rows(prev, m)), 0.0).astype(BF16)
                km = jnp.where(upper, 0.0, k * jnp.exp2(_spread_rows(last, m) - b)).astype(BF16)
                acc = _dot_nt(qm, km) + same_block(m) * acc
            return acc

        outs = []
        for n in range(C // S):
            lo_r, hi_r = n * S, (n + 1) * S
            blocks = [sub_scores(q[lo_r:hi_r], k[lo_r:hi_r], b[lo_r:hi_r])]
            if n > 0:
                edge = b[lo_r - 1:lo_r, :]
                qe = (q[lo_r:hi_r] * jnp.exp2(b[lo_r:hi_r] - edge)).astype(BF16)
                ke = (k[0:lo_r] * jnp.exp2(edge - b[0:lo_r])).astype(BF16)
                blocks.insert(0, _dot_nt(qe, ke))
            outs.append(_dot(jnp.concatenate(blocks, axis=1).astype(BF16), iv[0:hi_r].astype(BF16)))
        o_intra = jnp.concatenate(outs, axis=0)

        st = st_scr[hh]
        o = o_intra + _dot_nt((q * jnp.exp2(b)).astype(BF16), st.astype(BF16))
        b_last = b[C - 1:C, :]
        kh = (k * jnp.exp2(b_last - b)).astype(BF16)
        st_scr[hh] = st * jnp.exp2(b_last) + _dot(iv.T.astype(BF16), kh)

        o = o * lax.rsqrt(jnp.mean(o * o, axis=-1, keepdims=True) + EPS) * gn
        o = o * (g * jax.nn.sigmoid(g))
        o_ref[0, rows, hh * dh:(hh + 1) * dh] = o.astype(o_ref.dtype)

    def step(c, carry):
        rows = pl.ds(pl.multiple_of(c * C, C), C)
        for hh in range(HGRN_HEADS_PER_STEP):
            head_step(rows, hh)
        return carry

    lax.fori_loop(0, seq // C, step, 0)


def _hgrn_mixer(h3, gain, w_in, lower_raw, gate_norm, layer_j):
    B, L, D = h3.shape
    dh = HGRN_HEAD_DIM
    H = D // dh
    n_layers = lower_raw.shape[0]
    hp = HGRN_HEADS_PER_STEP
    assert H % hp == 0
    steps = H // hp
    w = w_in.astype(BF16)

    def part_spec(part):
        return pl.BlockSpec((D, hp * dh), lambda b, h: (0, part * steps + h))
    return pl.pallas_call(
        functools.partial(_hgrn_kernel, layer_j=layer_j, seq=L),
        grid=(B, steps),
        in_specs=[
            pl.BlockSpec((1, L, D), lambda b, h: (b, 0, 0)),
            pl.BlockSpec((1, D), lambda b, h: (0, 0)),
            part_spec(0), part_spec(1), part_spec(2), part_spec(3),
            pl.BlockSpec((n_layers, hp * dh), lambda b, h: (0, h)),
            pl.BlockSpec((1, dh), lambda b, h: (0, 0)),
        ],
        out_specs=pl.BlockSpec((1, L, hp * dh), lambda b, h: (b, 0, h)),
        out_shape=jax.ShapeDtypeStruct((B, L, D), BF16),
        scratch_shapes=[pltpu.VMEM((L, D), BF16), pltpu.VMEM((L, hp * 4 * dh), F32),
                        pltpu.VMEM((hp, dh, dh), F32)],
        compiler_params=_params(2),
        name="hgrn_mixer",
    )(h3, gain.reshape(1, D), w, w, w, w, lower_raw, gate_norm.reshape(1, dh))


def _dsa_proj_kernel(x_ref, gain_ref, w_ref, q_ref, k_ref, vt_ref, qi_ref, ki_ref, wt_ref):
    nq = ATT_HEADS * ATT_HEAD_DIM
    nkv = ATT_KV_HEADS * ATT_HEAD_DIM
    nqi = IDX_HEADS * IDX_HEAD_DIM
    xn = _rms_norm(x_ref[0], gain_ref[...]).astype(BF16)
    p = _dot(xn, w_ref[...])
    c0 = 0
    q_ref[0] = (p[:, c0:c0 + nq] * (ATT_HEAD_DIM ** -0.5 * LOG2E)).astype(BF16)
    c0 += nq
    k_ref[0] = p[:, c0:c0 + nkv].astype(BF16)
    c0 += nkv
    vt = p[:, c0:c0 + nkv].T
    tail = (lax.broadcasted_iota(I32, (ATT_V_ROWS - ATT_HEAD_DIM, vt.shape[1]), 0) == 0).astype(F32)
    vt_ref[0] = jnp.concatenate(
        [blk for g in range(ATT_KV_HEADS) for blk in (vt[g * ATT_HEAD_DIM:(g + 1) * ATT_HEAD_DIM, :], tail)],
        axis=0).astype(BF16)
    c0 += nkv
    qi_ref[0] = p[:, c0:c0 + nqi].astype(BF16)
    c0 += nqi
    ki_ref[0] = p[:, c0:c0 + IDX_HEAD_DIM].astype(BF16)
    tail = p[:, c0:c0 + 128].T
    wt_ref[0] = tail[IDX_HEAD_DIM:IDX_HEAD_DIM + IDX_HEADS, :] * (IDX_HEADS ** -0.5 * IDX_HEAD_DIM ** -0.5)


def _bias_kernel(rb_ref, out_ref):
    kb = KEY_BLOCK
    s = lax.broadcasted_iota(I32, (kb, kb), 0)
    t = lax.broadcasted_iota(I32, (kb, kb), 1)
    max_exact = REL_BUCKETS // 2
    for tile in range(2):
        n = jnp.maximum(t - s + kb * tile, 0)
        nf = jnp.maximum(n, max_exact).astype(F32)
        large = max_exact + (jnp.log(nf / max_exact) / math.log(REL_MAX_DIST / max_exact)
                             * (REL_BUCKETS - max_exact)).astype(I32)
        large = jnp.minimum(large, REL_BUCKETS - 1)
        bucket = jnp.where(n < max_exact, n, large)
        for h in range(ATT_HEADS):
            def pick(bk, acc):
                return jnp.where(bucket == bk, rb_ref[bk, h], acc)
            tile_h = lax.fori_loop(0, REL_BUCKETS, pick, jnp.zeros((kb, kb), F32))
            out_ref[h // ATT_GROUP, tile, :, (h % ATT_GROUP) * kb:(h % ATT_GROUP + 1) * kb] = (
                (tile_h - rb_ref[REL_BUCKETS - 1, h]) * LOG2E)
    for g in range(ATT_KV_HEADS):
        out_ref[g, 2] = jnp.zeros((kb, ATT_GROUP * kb), F32)


def _dsa_attn_kernel(q_ref, k_ref, vt_ref, qi_ref, ki_ref, wt_ref, bias_ref, o_ref,
                     qs_scr, qis_scr, key_scr, half_scr, mask_scr, s0_scr, s1_scr, acc_scr, out_scr, thr_scr,
                     *, top_k, seq):
    tq = Q_BLOCK
    kb = KEY_BLOCK
    sb = KEY_STEP
    nsub = tq // kb
    hd = ATT_HEAD_DIM
    sw = ATT_GROUP * kb
    gw = nsub * sw
    i = pl.program_id(1)
    n_steps = i + 1

    for g in range(ATT_KV_HEADS):
        for th in range(nsub):
            for hh in range(ATT_GROUP):
                r0 = g * gw + th * sw + hh * kb
                c0 = (g * ATT_GROUP + hh) * hd
                qs_scr[r0:r0 + kb, :] = q_ref[0, th * kb:(th + 1) * kb, c0:c0 + hd]
    for h in range(IDX_HEADS):
        qis_scr[h * tq:(h + 1) * tq, :] = qi_ref[0, :, h * IDX_HEAD_DIM:(h + 1) * IDX_HEAD_DIM]
    wt = wt_ref[0]

    s_iota = lax.broadcasted_iota(I32, (sb, tq), 0)
    t_glob = i * tq + lax.broadcasted_iota(I32, (sb, tq), 1)

    def step_rows(j):
        return pl.ds(pl.multiple_of(j * sb, sb), sb)

    def fold(x, op):
        return op(x.reshape(sb // 8, 8, x.shape[1]), axis=0)

    def idx_step(j, carry):
        rows = step_rows(j)
        d = _dot_nt(ki_ref[0, rows, :], qis_scr[...])
        sc = jnp.zeros((sb, tq), F32)
        for h in range(IDX_HEADS):
            sc = sc + wt[h:h + 1, :] * jnp.maximum(d[:, h * tq:(h + 1) * tq], 0.0)
        sc = jnp.where(j * sb + s_iota > t_glob, NEG_BIG, sc) + 0.0
        bits = pltpu.bitcast(sc, I32)
        key = jnp.where(bits < 0, bits ^ 0x7FFFFFFF, bits)
        key_scr[rows, :] = key
        half_scr[rows, :] = (key >> 16).astype(I16)
        return carry
    _quad_loop(0, n_steps, idx_step, 0)

    def count(pred):
        def body(j, cnt):
            hit = pred(key_scr[step_rows(j), :], j * sb + s_iota)
            return cnt + fold(jnp.where(hit, 1, 0), jnp.sum)
        return jnp.sum(_pair_loop(0, n_steps, body, jnp.zeros((8, tq), I32)), axis=0, keepdims=True)

    def count_half(preds):
        one, nil = jnp.ones((), I16), jnp.zeros((), I16)

        def body(j, cnts):
            v = half_scr[step_rows(j), :]
            out = []
            for pred, cnt in zip(preds, cnts):
                ones = jnp.where(pred(v), one, nil)
                for r in range(sb // 16):
                    cnt = cnt + ones[r * 16:(r + 1) * 16, :]
                out.append(cnt)
            return tuple(out)
        cnts = _pair_loop(0, n_steps, body, (jnp.zeros((16, tq), I16),) * len(preds))
        return [jnp.sum(c.astype(I32), axis=0, keepdims=True) for c in cnts]

    def at_least(cand):
        c16 = cand.astype(I16)
        return lambda v: v >= c16

    def kth_largest_half(target):
        c0, = count_half([at_least(jnp.zeros((1, tq), I32))])
        t0 = jnp.where(c0 >= target, 0, -(2 ** 15)).astype(I32)

        def bit_step(it, t_cur):
            cand = t_cur | jnp.left_shift(jnp.int32(1), 14 - it)
            cnt, = count_half([at_least(cand)])
            return jnp.where(cnt >= target, cand, t_cur)
        return lax.fori_loop(0, 15, bit_step, t0)

    thr_scr[0:1, :] = jnp.full((1, tq), INT_MIN, I32)
    thr_scr[1:2, :] = jnp.full((1, tq), 2 ** 31 - 1, I32)

    @pl.when((i + 1) * tq > top_k)
    def _():
        t_hi = kth_largest_half(top_k)
        n_above, = count_half([lambda v: v > t_hi.astype(I16)])
        need_lo = top_k - n_above

        def low_step(j, carry):
            rows = step_rows(j)
            key = key_scr[rows, :]
            low = (key & 0xFFFF) - 2 ** 15
            half_scr[rows, :] = jnp.where((key >> 16) == t_hi, low, -(2 ** 15)).astype(I16)
            return carry
        _pair_loop(0, n_steps, low_step, 0)
        t_lo = kth_largest_half(need_lo)
        thr = (t_hi << 16) | (t_lo + 2 ** 15)
        thr_scr[0:1, :] = thr

        n_ge = count(lambda key, s: key >= thr)

        @pl.when(jnp.max(n_ge) > top_k)
        def _():
            need = top_k - count(lambda key, s: key > thr)

            pos_bits = (seq - 1).bit_length()

            def pos_step(it, j_cur):
                cand = j_cur | jnp.left_shift(jnp.int32(1), pos_bits - 1 - it)
                cnt = count(lambda key, s: (key == thr) & (s < cand))
                return jnp.where(cnt < need, cand, j_cur)
            thr_scr[1:2, :] = lax.fori_loop(0, pos_bits, pos_step, jnp.zeros((1, tq), I32))

    thr = thr_scr[0:1, :]
    j_last = thr_scr[1:2, :]

    def mask_step(j, carry):
        rows = step_rows(j)
        key = key_scr[rows, :]
        s_glob = j * sb + s_iota
        sel = ((key > thr) | ((key == thr) & (s_glob <= j_last))) & (s_glob <= t_glob)
        mask_scr[rows, :] = jnp.where(sel, 0.0, NEG_BIG)
        return carry
    _pair_loop(0, n_steps, mask_step, 0)

    s_bufs = (s0_scr, s1_scr)
    far_steps = jnp.maximum(i - 1, 0)
    diag_tiles = ((0, 1), (2, 0))
    prev_tiles = ((2, 2), (1, 2))

    def logits(g, j, band):
        rows = step_rows(j)
        s = _dot_nt(k_ref[0, rows, g * hd:(g + 1) * hd], qs_scr[g * gw:(g + 1) * gw, :])
        mk = mask_scr[rows, :]
        s = s + jnp.concatenate([mk[:, th * kb:(th + 1) * kb] for th in range(nsub) for _ in range(ATT_GROUP)],
                                axis=1)
        if band:
            s = jnp.concatenate([
                jnp.concatenate([
                    s[sh * kb:(sh + 1) * kb, th * sw:(th + 1) * sw]
                    + bias_ref[g, jnp.where(j == i, diag_tiles[sh][th], prev_tiles[sh][th])]
                    for th in range(nsub)], axis=1)
                for sh in range(nsub)], axis=0)
        s_bufs[g % 2][rows, :] = s
        return fold(s, jnp.max)

    def exp_pv(g, j, m, carry):
        rows = step_rows(j)
        p = jnp.exp2(s_bufs[g % 2][rows, :] - m)
        acc_scr[...] += _dot(vt_ref[0, g * ATT_V_ROWS:(g + 1) * ATT_V_ROWS, rows], p.astype(BF16))
        return carry

    def finish(g):
        res = acc_scr[0:hd, :] / acc_scr[hd:hd + 1, :]
        for th in range(nsub):
            for hh in range(ATT_GROUP):
                h = g * ATT_GROUP + hh
                out_scr[h * hd:(h + 1) * hd, th * kb:(th + 1) * kb] = res[:, th * sw + hh * kb:th * sw + (hh + 1) * kb]

    neg = jnp.full((8, gw), NEG_BIG, F32)
    m8 = _quad_loop(0, far_steps, lambda j, c: jnp.maximum(c, logits(0, j, False)), neg)
    m8 = lax.fori_loop(far_steps, n_steps, lambda j, c: jnp.maximum(c, logits(0, j, True)), m8)
    for g in range(ATT_KV_HEADS):
        m = jnp.max(m8, axis=0, keepdims=True)
        acc_scr[...] = jnp.zeros_like(acc_scr)
        if g + 1 < ATT_KV_HEADS:
            def both(j, m8n, band):
                exp_pv(g, j, m, 0)
                return jnp.maximum(m8n, logits(g + 1, j, band))
            m8 = _quad_loop(0, far_steps, functools.partial(both, band=False), neg)
            m8 = lax.fori_loop(far_steps, n_steps, functools.partial(both, band=True), m8)
        else:
            _quad_loop(0, n_steps, lambda j, c: exp_pv(g, j, m, c), 0)
        finish(g)

    o_ref[0] = out_scr[...].T.astype(o_ref.dtype)


def _dsa_mixer(h3, gain, w_in, rel_bias):
    B, L, D = h3.shape
    top_k = min(TOPK_MAX, L // 4)
    assert L % KEY_STEP == 0 and top_k % KEY_BLOCK == 0
    tm = 512
    nq = ATT_HEADS * ATT_HEAD_DIM
    nkv = ATT_KV_HEADS * ATT_HEAD_DIM
    nqi = IDX_HEADS * IDX_HEAD_DIM
    nvt = ATT_KV_HEADS * ATT_V_ROWS
    width = w_in.shape[1]
    padded = nq + 2 * nkv + nqi + 128
    w = jnp.pad(w_in, ((0, 0), (0, padded - width))).astype(BF16)
    q, k, vt, qi, ki, wt = pl.pallas_call(
        _dsa_proj_kernel,
        grid=(B, L // tm),
        in_specs=[
            pl.BlockSpec((1, tm, D), lambda b, r: (b, r, 0)),
            pl.BlockSpec((1, D), lambda b, r: (0, 0)),
            pl.BlockSpec((D, padded), lambda b, r: (0, 0)),
        ],
        out_specs=[
            pl.BlockSpec((1, tm, nq), lambda b, r: (b, r, 0)),
            pl.BlockSpec((1, tm, nkv), lambda b, r: (b, r, 0)),
            pl.BlockSpec((1, nvt, tm), lambda b, r: (b, 0, r)),
            pl.BlockSpec((1, tm, nqi), lambda b, r: (b, r, 0)),
            pl.BlockSpec((1, tm, IDX_HEAD_DIM), lambda b, r: (b, r, 0)),
            pl.BlockSpec((1, IDX_HEADS, tm), lambda b, r: (b, 0, r)),
        ],
        out_shape=[
            jax.ShapeDtypeStruct((B, L, nq), BF16),
            jax.ShapeDtypeStruct((B, L, nkv), BF16),
            jax.ShapeDtypeStruct((B, nvt, L), BF16),
            jax.ShapeDtypeStruct((B, L, nqi), BF16),
            jax.ShapeDtypeStruct((B, L, IDX_HEAD_DIM), BF16),
            jax.ShapeDtypeStruct((B, IDX_HEADS, L), F32),
        ],
        compiler_params=_params(2),
        name="dsa_proj",
    )(h3, gain.reshape(1, D), w)

    bias = pl.pallas_call(
        _bias_kernel,
        in_specs=[pl.BlockSpec(memory_space=pltpu.SMEM)],
        out_specs=pl.BlockSpec(memory_space=pltpu.VMEM),
        out_shape=jax.ShapeDtypeStruct((ATT_KV_HEADS, BIAS_TILES, KEY_BLOCK, ATT_GROUP * KEY_BLOCK), F32),
        name="dsa_bias_tiles",
    )(rel_bias)

    tq = Q_BLOCK
    return pl.pallas_call(
        functools.partial(_dsa_attn_kernel, top_k=top_k, seq=L),
        grid=(B, L // tq),
        in_specs=[
            pl.BlockSpec((1, tq, nq), lambda b, i: (b, i, 0)),
            pl.BlockSpec((1, L, nkv), lambda b, i: (b, 0, 0)),
            pl.BlockSpec((1, nvt, L), lambda b, i: (b, 0, 0)),
            pl.BlockSpec((1, tq, nqi), lambda b, i: (b, i, 0)),
            pl.BlockSpec((1, L, IDX_HEAD_DIM), lambda b, i: (b, 0, 0)),
            pl.BlockSpec((1, IDX_HEADS, tq), lambda b, i: (b, 0, i)),
            pl.BlockSpec((ATT_KV_HEADS, BIAS_TILES, KEY_BLOCK, ATT_GROUP * KEY_BLOCK), lambda b, i: (0, 0, 0, 0)),
        ],
        out_specs=pl.BlockSpec((1, tq, D), lambda b, i: (b, i, 0)),
        out_shape=jax.ShapeDtypeStruct((B, L, D), BF16),
        scratch_shapes=[
            pltpu.VMEM((ATT_HEADS * tq, ATT_HEAD_DIM), BF16),
            pltpu.VMEM((IDX_HEADS * tq, IDX_HEAD_DIM), BF16),
            pltpu.VMEM((L, tq), I32),
            pltpu.VMEM((L, tq), I16),
            pltpu.VMEM((L, tq), F32),
            pltpu.VMEM((L, ATT_GROUP * tq), F32),
            pltpu.VMEM((L, ATT_GROUP * tq), F32),
            pltpu.VMEM((ATT_V_ROWS, ATT_GROUP * tq), F32),
            pltpu.VMEM((D, tq), F32),
            pltpu.VMEM((8, tq), I32),
        ],
        compiler_params=_params(2),
        name="dsa_attention",
    )(q, k, vt, qi, ki, wt, bias)


def _ffn_kernel(h_ref, o_ref, wo_ref, gain_ref, wup_ref, cw_ref, cb_ref, wdn_ref, fgain_ref, out_ref,
                xn_scr, carry_scr, acc_scr, act_scr, *, tiles_per_seq, final):
    tm = h_ref.shape[0]
    d_ff = wdn_ref.shape[0]
    tn = FFN_TILE
    n_ff = d_ff // tn
    i = pl.program_id(0)

    hnew = h_ref[...] + _dot(o_ref[...], wo_ref[...])
    acc_scr[...] = hnew
    xn_scr[...] = _rms_norm(hnew, gain_ref[...]).astype(BF16)
    first = (i % tiles_per_seq) == 0

    @pl.when(i == 0)
    def _():
        carry_scr[...] = jnp.zeros_like(carry_scr)

    top = lax.broadcasted_iota(I32, (8, tn), 0)

    def up(j):
        out = []
        for c0 in (j * tn, d_ff + j * tn):
            hu = _dot(xn_scr[...], wup_ref[:, c0:c0 + tn])
            out.append((hu, jnp.where(first, 0.0, carry_scr[:, c0:c0 + tn]), c0))
            carry_scr[:, c0:c0 + tn] = hu[tm - 8:tm, :]
        return out

    def shifted(hu, prev, d):
        r = pltpu.roll(hu, d, axis=0)
        head = jnp.where(top < d, pltpu.roll(prev, d, axis=0), r[0:8, :])
        return jnp.concatenate([head, r[8:, :]], axis=0)

    def down(j, tile):
        ys = []
        for hu, prev, c0 in tile:
            cw = cw_ref[:, c0:c0 + tn]
            ys.append(cw[2:3, :] * hu + cw[1:2, :] * shifted(hu, prev, 1) + cw[0:1, :] * shifted(hu, prev, 2)
                      + cb_ref[:, c0:c0 + tn])
        gate, upv = ys
        act_scr[:, j * tn:(j + 1) * tn] = (gate * jax.nn.sigmoid(gate) * upv).astype(BF16)

    tile = up(0)
    for j in range(n_ff):
        nxt = up(j + 1) if j + 1 < n_ff else None
        down(j, tile)
        tile = nxt

    res = acc_scr[...] + _dot(act_scr[...], wdn_ref[...])
    if final:
        res = _rms_norm(res, fgain_ref[...])
    out_ref[...] = res


def _ffn(h2, o2, w_out, gain, w_up, conv_w, conv_b, w_down, final_gain, *, seq, final):
    M, D = h2.shape
    F = w_down.shape[0]
    tm = FFN_ROWS
    tn = FFN_TILE
    assert F % tn == 0 and seq % tm == 0
    const2 = lambda i: (0, 0)
    resident = dict(pipeline_mode=pl.Buffered(1))
    return pl.pallas_call(
        functools.partial(_ffn_kernel, tiles_per_seq=seq // tm, final=final),
        grid=(M // tm,),
        in_specs=[
            pl.BlockSpec((tm, D), lambda i: (i, 0)),
            pl.BlockSpec((tm, D), lambda i: (i, 0)),
            pl.BlockSpec((D, D), const2, **resident),
            pl.BlockSpec((1, D), const2),
            pl.BlockSpec((D, 2 * F), const2, **resident),
            pl.BlockSpec((3, 2 * F), const2),
            pl.BlockSpec((1, 2 * F), const2),
            pl.BlockSpec((F, D), const2, **resident),
            pl.BlockSpec((1, D), const2),
        ],
        out_specs=pl.BlockSpec((tm, D), lambda i: (i, 0)),
        out_shape=jax.ShapeDtypeStruct((M, D), F32),
        scratch_shapes=[
            pltpu.VMEM((tm, D), BF16),
            pltpu.VMEM((8, 2 * F), F32),
            pltpu.VMEM((tm, D), F32),
            pltpu.VMEM((tm, F), BF16),
        ],
        compiler_params=_params(1),
        name="outproj_convffn",
    )(h2, o2, w_out.astype(BF16), gain.reshape(1, D), w_up.astype(BF16), conv_w, conv_b.reshape(1, 2 * F),
      w_down.astype(BF16), final_gain.reshape(1, D))


def kernel(x, attn_norm, ffn_norm, hgrn_w_in, hgrn_w_out, hgrn_gate_norm, hgrn_lower_bounds, dsa_w_in, dsa_w_out, rel_bias, ffn_w_up, ffn_conv_w, ffn_conv_b, ffn_w_down, final_norm):
    B, L, D = x.shape
    depth = attn_norm.shape[0]
    h = x.reshape(B * L, D)
    for layer in range(depth):
        j = layer // 2
        h3 = h.reshape(B, L, D)
        if layer % 2 == 0:
            o = _hgrn_mixer(h3, attn_norm[layer], hgrn_w_in[j], hgrn_lower_bounds, hgrn_gate_norm[j], j)
            w_out = hgrn_w_out[j]
        else:
            o = _dsa_mixer(h3, attn_norm[layer], dsa_w_in[j], rel_bias)
            w_out = dsa_w_out[j]
        h = _ffn(h, o.reshape(B * L, D), w_out, ffn_norm[layer], ffn_w_up[layer], ffn_conv_w[layer],
                 ffn_conv_b[layer], ffn_w_down[layer], final_norm, seq=L, final=(layer == depth - 1))
    return h.reshape(B, L, D)
```

```python
import functools
import math

import jax
import jax.numpy as jnp
from jax import lax
from jax.experimental import pallas as pl
from jax.experimental.pallas import tpu as pltpu

F32 = jnp.float32
BF16 = jnp.bfloat16
I32 = jnp.int32
I16 = jnp.int16

EPS = 1e-6
NEG_BIG = -1e30
TINY = 1e-30

HGRN_HEAD_DIM = 128
HGRN_BLOCK = 256
HGRN_SUB = HGRN_BLOCK
HGRN_HEADS_PER_STEP = 4
ATT_HEADS = 16
ATT_HEAD_DIM = 64
ATT_KV_HEADS = 2
ATT_GROUP = ATT_HEADS // ATT_KV_HEADS
ATT_V_ROWS = ATT_HEAD_DIM + 16
IDX_HEADS = 8
IDX_HEAD_DIM = 64
TOPK_MAX = 256
REL_BUCKETS = 32
REL_MAX_DIST = 128
KEY_BLOCK = 128
Q_BLOCK = 2 * KEY_BLOCK
KEY_STEP = Q_BLOCK
BIAS_TILES = 2
FFN_ROWS = 512
FFN_TILE = 256
INT_MIN = -(2 ** 31)
LOG2E = 1.4426950408889634

VMEM_LIMIT_BYTES = 56 * 1024 * 1024


def _dot(a, b):
    return jnp.dot(a, b, preferred_element_type=F32)


def _dot_nt(a, b):
    return lax.dot_general(a, b, (((1,), (1,)), ((), ())), preferred_element_type=F32)


def _rms_norm(x, gain):
    y = x * lax.rsqrt(jnp.mean(x * x, axis=-1, keepdims=True) + EPS)
    return y * gain


def _pair_loop(lo, hi, body, init):
    def two(p, c):
        j = lo + 2 * p
        return body(j + 1, body(j, c))
    c = lax.fori_loop(0, (hi - lo) // 2, two, init)
    return lax.cond((hi - lo) % 2 == 1, lambda c: body(hi - 1, c), lambda c: c, c)


def _quad_loop(lo, hi, body, init):
    n = hi - lo

    def run(start, width, c):
        for d in range(width):
            c = body(start + d, c)
        return c
    c = lax.fori_loop(0, n // 4, lambda p, c: run(lo + 4 * p, 4, c), init)
    done = lo + (n // 4) * 4
    c = lax.cond((n & 2) != 0, lambda c: run(done, 2, c), lambda c: c, c)
    return lax.cond((n & 1) != 0, lambda c: body(hi - 1, c), lambda c: c, c)


def _params(n_axes):
    return pltpu.CompilerParams(dimension_semantics=("arbitrary",) * n_axes,
                                vmem_limit_bytes=VMEM_LIMIT_BYTES)


def _block_rows(a, m):
    c = a.shape[0]
    return a.reshape(c // m, m, a.shape[1])[:, m - 1:m, :]


def _spread_rows(r, m):
    n = r.shape[0]
    return jnp.broadcast_to(r, (n, m, r.shape[2])).reshape(n * m, r.shape[2])


def _hgrn_kernel(x_ref, gain_ref, wq_ref, wf_ref, wi_ref, wg_ref, lbraw_ref, gn_ref, o_ref,
                 xn_scr, proj_scr, st_scr, *, layer_j, seq):
    C = HGRN_BLOCK
    dh = HGRN_HEAD_DIM
    pw = HGRN_HEADS_PER_STEP * dh
    head = pl.program_id(1)

    @pl.when(head == 0)
    def _():
        rb = 256

        def norm_rows(r, carry):
            rows = pl.ds(pl.multiple_of(r * rb, rb), rb)
            xn_scr[rows, :] = _rms_norm(x_ref[0, rows, :], gain_ref[...]).astype(BF16)
            return carry
        lax.fori_loop(0, seq // rb, norm_rows, 0)

    pb = 512

    def proj_rows(r, carry):
        rows = pl.ds(pl.multiple_of(r * pb, pb), pb)
        xr = xn_scr[rows, :]
        for part, w_ref in enumerate((wq_ref, wf_ref, wi_ref, wg_ref)):
            proj_scr[rows, part * pw:(part + 1) * pw] = _dot(xr, w_ref[...])
        return carry
    lax.fori_loop(0, seq // pb, proj_rows, 0)

    raw = lbraw_ref[...]
    ex = jnp.exp(raw - jnp.max(raw, axis=0, keepdims=True))
    soft = ex / jnp.sum(ex, axis=0, keepdims=True)
    lb_all = jnp.zeros((1, raw.shape[1]), F32)
    for l in range(1, layer_j + 1):
        lb_all = lb_all + soft[l:l + 1, :]
    gn = gn_ref[...]

    S = HGRN_SUB
    tril = (lax.broadcasted_iota(I32, (C, C), 1) <= lax.broadcasted_iota(I32, (C, C), 0)).astype(BF16)
    ri = lax.broadcasted_iota(I32, (S, S), 0)
    ci = lax.broadcasted_iota(I32, (S, S), 1)
    trow = lax.broadcasted_iota(I32, (S, dh), 0)
    sub = trow & 7

    def same_block(n):
        sh = int(math.log2(n))
        return ((ri >> sh) == (ci >> sh)).astype(F32)

    st_scr[...] = jnp.zeros_like(st_scr)

    def head_step(rows, hh):
        c0 = hh * dh
        q = proj_scr[rows, c0:c0 + dh]
        f = proj_scr[rows, pw + c0:pw + c0 + dh]
        iv = proj_scr[rows, 2 * pw + c0:2 * pw + c0 + dh]
        g = proj_scr[rows, 3 * pw + c0:3 * pw + c0 + dh]
        lb = lb_all[:, hh * dh:(hh + 1) * dh]
        oml = 1.0 - lb

        e = jnp.exp(-jnp.abs(f))
        r = 1.0 / (1.0 + e)
        er = e * r
        pos = f >= 0.0
        forget = lb + oml * jnp.where(pos, r, er)
        logf = jnp.log(jnp.maximum(forget, TINY)) * LOG2E
        k = oml * jnp.where(pos, er, r)

        hi = logf.astype(BF16)
        r1 = logf - hi.astype(F32)
        mid = r1.astype(BF16)
        lo = (r1 - mid.astype(F32)).astype(BF16)
        b = _dot(tril, hi) + _dot(tril, mid) + _dot(tril, lo)

        def sub_scores(q, k, b):
            b8 = b.reshape(S // 8, 8, dh)
            qs, ks = [], []
            for rho in range(8):
                bref = _spread_rows(b8[:, rho:rho + 1, :], 8)
                qs.append(jnp.where(sub >= rho, q * jnp.exp2(b - bref), 0.0).astype(BF16))
                ks.append(jnp.where(sub == rho, k, 0.0).astype(BF16))
            acc = _dot_nt(jnp.concatenate(qs, axis=1), jnp.concatenate(ks, axis=1))
            for m in [8 << lv for lv in range(int(math.log2(S // 8)))]:
                last = _block_rows(b, m)
                prev = jnp.concatenate([jnp.zeros((1, 1, dh), F32), last[:-1]], axis=0)
                upper = ((trow >> int(math.log2(m))) & 1) == 1
                qm = jnp.where(upper, q * jnp.exp2(b - _spread_rows(prev, m)), 0.0).astype(BF16)
                km = jnp.where(upper, 0.0, k * jnp.exp2(_spread_rows(last, m) - b)).astype(BF16)
                acc = _dot_nt(qm, km) + same_block(m) * acc
            return acc

        outs = []
        for n in range(C // S):
            lo_r, hi_r = n * S, (n + 1) * S
            blocks = [sub_scores(q[lo_r:hi_r], k[lo_r:hi_r], b[lo_r:hi_r])]
            if n > 0:
                edge = b[lo_r - 1:lo_r, :]
                qe = (q[lo_r:hi_r] * jnp.exp2(b[lo_r:hi_r] - edge)).astype(BF16)
                ke = (k[0:lo_r] * jnp.exp2(edge - b[0:lo_r])).astype(BF16)
                blocks.insert(0, _dot_nt(qe, ke))
            outs.append(_dot(jnp.concatenate(blocks, axis=1).astype(BF16), iv[0:hi_r].astype(BF16)))
        o_intra = jnp.concatenate(outs, axis=0)

        st = st_scr[hh]
        o = o_intra + _dot_nt((q * jnp.exp2(b)).astype(BF16), st.astype(BF16))
        b_last = b[C - 1:C, :]
        kh = (k * jnp.exp2(b_last - b)).astype(BF16)
        st_scr[hh] = st * jnp.exp2(b_last) + _dot(iv.T.astype(BF16), kh)

        o = o * lax.rsqrt(jnp.mean(o * o, axis=-1, keepdims=True) + EPS) * gn
        o = o * (g * jax.nn.sigmoid(g))
        o_ref[0, rows, hh * dh:(hh + 1) * dh] = o.astype(o_ref.dtype)

    def step(c, carry):
        rows = pl.ds(pl.multiple_of(c * C, C), C)
        for hh in range(HGRN_HEADS_PER_STEP):
            head_step(rows, hh)
        return carry

    lax.fori_loop(0, seq // C, step, 0)


def _hgrn_mixer(h3, gain, w_in, lower_raw, gate_norm, layer_j):
    B, L, D = h3.shape
    dh = HGRN_HEAD_DIM
    H = D // dh
    n_layers = lower_raw.shape[0]
    hp = HGRN_HEADS_PER_STEP
    assert H % hp == 0
    steps = H // hp
    w = w_in.astype(BF16)

    def part_spec(part):
        return pl.BlockSpec((D, hp * dh), lambda b, h: (0, part * steps + h))
    return pl.pallas_call(
        functools.partial(_hgrn_kernel, layer_j=layer_j, seq=L),
        grid=(B, steps),
        in_specs=[
            pl.BlockSpec((1, L, D), lambda b, h: (b, 0, 0)),
            pl.BlockSpec((1, D), lambda b, h: (0, 0)),
            part_spec(0), part_spec(1), part_spec(2), part_spec(3),
            pl.BlockSpec((n_layers, hp * dh), lambda b, h: (0, h)),
            pl.BlockSpec((1, dh), lambda b, h: (0, 0)),
        ],
        out_specs=pl.BlockSpec((1, L, hp * dh), lambda b, h: (b, 0, h)),
        out_shape=jax.ShapeDtypeStruct((B, L, D), BF16),
        scratch_shapes=[pltpu.VMEM((L, D), BF16), pltpu.VMEM((L, hp * 4 * dh), F32),
                        pltpu.VMEM((hp, dh, dh), F32)],
        compiler_params=_params(2),
        name="hgrn_mixer",
    )(h3, gain.reshape(1, D), w, w, w, w, lower_raw, gate_norm.reshape(1, dh))


def _dsa_proj_kernel(x_ref, gain_ref, w_ref, q_ref, k_ref, vt_ref, qi_ref, ki_ref, wt_ref):
    nq = ATT_HEADS * ATT_HEAD_DIM
    nkv = ATT_KV_HEADS * ATT_HEAD_DIM
    nqi = IDX_HEADS * IDX_HEAD_DIM
    xn = _rms_norm(x_ref[0], gain_ref[...]).astype(BF16)
    p = _dot(xn, w_ref[...])
    c0 = 0
    q_ref[0] = (p[:, c0:c0 + nq] * (ATT_HEAD_DIM ** -0.5 * LOG2E)).astype(BF16)
    c0 += nq
    k_ref[0] = p[:, c0:c0 + nkv].astype(BF16)
    c0 += nkv
    vt = p[:, c0:c0 + nkv].T
    tail = (lax.broadcasted_iota(I32, (ATT_V_ROWS - ATT_HEAD_DIM, vt.shape[1]), 0) == 0).astype(F32)
    vt_ref[0] = jnp.concatenate(
        [blk for g in range(ATT_KV_HEADS) for blk in (vt[g * ATT_HEAD_DIM:(g + 1) * ATT_HEAD_DIM, :], tail)],
        axis=0).astype(BF16)
    c0 += nkv
    qi_ref[0] = p[:, c0:c0 + nqi].astype(BF16)
    c0 += nqi
    ki_ref[0] = p[:, c0:c0 + IDX_HEAD_DIM].astype(BF16)
    tail = p[:, c0:c0 + 128].T
    wt_ref[0] = tail[IDX_HEAD_DIM:IDX_HEAD_DIM + IDX_HEADS, :] * (IDX_HEADS ** -0.5 * IDX_HEAD_DIM ** -0.5)


def _bias_kernel(rb_ref, out_ref):
    kb = KEY_BLOCK
    s = lax.broadcasted_iota(I32, (kb, kb), 0)
    t = lax.broadcasted_iota(I32, (kb, kb), 1)
    max_exact = REL_BUCKETS // 2
    for tile in range(2):
        n = jnp.maximum(t - s + kb * tile, 0)
        nf = jnp.maximum(n, max_exact).astype(F32)
        large = max_exact + (jnp.log(nf / max_exact) / math.log(REL_MAX_DIST / max_exact)
                             * (REL_BUCKETS - max_exact)).astype(I32)
        large = jnp.minimum(large, REL_BUCKETS - 1)
        bucket = jnp.where(n < max_exact, n, large)
        for h in range(ATT_HEADS):
            def pick(bk, acc):
                return jnp.where(bucket == bk, rb_ref[bk, h], acc)
            tile_h = lax.fori_loop(0, REL_BUCKETS, pick, jnp.zeros((kb, kb), F32))
            out_ref[h // ATT_GROUP, tile, :, (h % ATT_GROUP) * kb:(h % ATT_GROUP + 1) * kb] = (
                (tile_h - rb_ref[REL_BUCKETS - 1, h]) * LOG2E)


def _dsa_attn_kernel(q_ref, k_ref, vt_ref, qi_ref, ki_ref, wt_ref, bias_ref, o_ref,
                     qs_scr, qis_scr, key_scr, half_scr, mask_scr, s0_scr, s1_scr, acc_scr, out_scr, thr_scr,
                     *, top_k, seq):
    tq = Q_BLOCK
    kb = KEY_BLOCK
    sb = KEY_STEP
    nsub = tq // kb
    hd = ATT_HEAD_DIM
    sw = ATT_GROUP * kb
    gw = nsub * sw
    i = pl.program_id(1)
    n_steps = i + 1

    for g in range(ATT_KV_HEADS):
        for th in range(nsub):
            for hh in range(ATT_GROUP):
                r0 = g * gw + th * sw + hh * kb
                c0 = (g * ATT_GROUP + hh) * hd
                qs_scr[r0:r0 + kb, :] = q_ref[0, th * kb:(th + 1) * kb, c0:c0 + hd]
    for h in range(IDX_HEADS):
        qis_scr[h * tq:(h + 1) * tq, :] = qi_ref[0, :, h * IDX_HEAD_DIM:(h + 1) * IDX_HEAD_DIM]
    wt = wt_ref[0]

    s_iota = lax.broadcasted_iota(I32, (sb, tq), 0)
    t_glob = i * tq + lax.broadcasted_iota(I32, (sb, tq), 1)

    def step_rows(j):
        return pl.ds(pl.multiple_of(j * sb, sb), sb)

    def fold(x, op):
        return op(x.reshape(sb // 8, 8, x.shape[1]), axis=0)

    def idx_step(j, carry):
        rows = step_rows(j)
        d = _dot_nt(ki_ref[0, rows, :], qis_scr[...])
        sc = jnp.zeros((sb, tq), F32)
        for h in range(IDX_HEADS):
            sc = sc + wt[h:h + 1, :] * jnp.maximum(d[:, h * tq:(h + 1) * tq], 0.0)
        sc = jnp.where(j * sb + s_iota > t_glob, NEG_BIG, sc) + 0.0
        bits = pltpu.bitcast(sc, I32)
        key = jnp.where(bits < 0, bits ^ 0x7FFFFFFF, bits)
        key_scr[rows, :] = key
        half_scr[rows, :] = (key >> 16).astype(I16)
        return carry
    _quad_loop(0, n_steps, idx_step, 0)

    def count(pred):
        def body(j, cnt):
            hit = pred(key_scr[step_rows(j), :], j * sb + s_iota)
            return cnt + fold(jnp.where(hit, 1, 0), jnp.sum)
        return jnp.sum(_pair_loop(0, n_steps, body, jnp.zeros((8, tq), I32)), axis=0, keepdims=True)

    def count_half(preds):
        one, nil = jnp.ones((), I16), jnp.zeros((), I16)

        def body(j, cnts):
            v = half_scr[step_rows(j), :]
            out = []
            for pred, cnt in zip(preds, cnts):
                ones = jnp.where(pred(v), one, nil)
                for r in range(sb // 16):
                    cnt = cnt + ones[r * 16:(r + 1) * 16, :]
                out.append(cnt)
            return tuple(out)
        cnts = _pair_loop(0, n_steps, body, (jnp.zeros((16, tq), I16),) * len(preds))
        return [jnp.sum(c.astype(I32), axis=0, keepdims=True) for c in cnts]

    def at_least(cand):
        c16 = cand.astype(I16)
        return lambda v: v >= c16

    def kth_largest_half(target):
        c0, = count_half([at_least(jnp.zeros((1, tq), I32))])
        t0 = jnp.where(c0 >= target, 0, -(2 ** 15)).astype(I32)

        def bit_step(it, t_cur):
            cand = t_cur | jnp.left_shift(jnp.int32(1), 14 - it)
            cnt, = count_half([at_least(cand)])
            return jnp.where(cnt >= target, cand, t_cur)
        return lax.fori_loop(0, 15, bit_step, t0)

    thr_scr[0:1, :] = jnp.full((1, tq), INT_MIN, I32)
    thr_scr[1:2, :] = jnp.full((1, tq), 2 ** 31 - 1, I32)

    @pl.when((i + 1) * tq > top_k)
    def _():
        t_hi = kth_largest_half(top_k)
        n_above, = count_half([lambda v: v > t_hi.astype(I16)])
        need_lo = top_k - n_above

        def low_step(j, carry):
            rows = step_rows(j)
            key = key_scr[rows, :]
            low = (key & 0xFFFF) - 2 ** 15
            half_scr[rows, :] = jnp.where((key >> 16) == t_hi, low, -(2 ** 15)).astype(I16)
            return carry
        _pair_loop(0, n_steps, low_step, 0)
        t_lo = kth_largest_half(need_lo)
        thr = (t_hi << 16) | (t_lo + 2 ** 15)
        thr_scr[0:1, :] = thr

        n_ge = count(lambda key, s: key >= thr)

        @pl.when(jnp.max(n_ge) > top_k)
        def _():
            need = top_k - count(lambda key, s: key > thr)

            pos_bits = (seq - 1).bit_length()

            def pos_step(it, j_cur):
                cand = j_cur | jnp.left_shift(jnp.int32(1), pos_bits - 1 - it)
                cnt = count(lambda key, s: (key == thr) & (s < cand))
                return jnp.where(cnt < need, cand, j_cur)
            thr_scr[1:2, :] = lax.fori_loop(0, pos_bits, pos_step, jnp.zeros((1, tq), I32))

    thr = thr_scr[0:1, :]
    j_last = thr_scr[1:2, :]

    def mask_step(j, carry):
        rows = step_rows(j)
        key = key_scr[rows, :]
        s_glob = j * sb + s_iota
        sel = ((key > thr) | ((key == thr) & (s_glob <= j_last))) & (s_glob <= t_glob)
        mask_scr[rows, :] = jnp.where(sel, 0.0, NEG_BIG)
        return carry
    _pair_loop(0, n_steps, mask_step, 0)

    s_bufs = (s0_scr, s1_scr)
    far_steps = jnp.maximum(i - 1, 0)
    band_tiles = {"diag": ((0, 1), (None, 0)), "prev": ((None, None), (1, None)), None: None}

    def logits(g, j, band):
        tiles = band_tiles[band]
        rows = step_rows(j)
        s = _dot_nt(k_ref[0, rows, g * hd:(g + 1) * hd], qs_scr[g * gw:(g + 1) * gw, :])
        mk = mask_scr[rows, :]
        s = s + jnp.concatenate([mk[:, th * kb:(th + 1) * kb] for th in range(nsub) for _ in range(ATT_GROUP)],
                                axis=1)
        if tiles is not None:
            def piece(sh, th):
                blk = s[sh * kb:(sh + 1) * kb, th * sw:(th + 1) * sw]
                return blk if tiles[sh][th] is None else blk + bias_ref[g, tiles[sh][th]]
            s = jnp.concatenate([jnp.concatenate([piece(sh, th) for th in range(nsub)], axis=1)
                                 for sh in range(nsub)], axis=0)
        s_bufs[g % 2][rows, :] = s
        return fold(s, jnp.max)

    def exp_pv(g, j, m, carry):
        rows = step_rows(j)
        p = jnp.exp2(s_bufs[g % 2][rows, :] - m)
        acc_scr[...] += _dot(vt_ref[0, g * ATT_V_ROWS:(g + 1) * ATT_V_ROWS, rows], p.astype(BF16))
        return carry

    def finish(g):
        res = acc_scr[0:hd, :] / acc_scr[hd:hd + 1, :]
        for th in range(nsub):
            for hh in range(ATT_GROUP):
                h = g * ATT_GROUP + hh
                out_scr[h * hd:(h + 1) * hd, th * kb:(th + 1) * kb] = res[:, th * sw + hh * kb:th * sw + (hh + 1) * kb]

    def band_steps(step, carry):
        return lax.cond(i >= 1, lambda c: step(i, "diag", step(i - 1, "prev", c)),
                        lambda c: step(i, "diag", c), carry)

    neg = jnp.full((8, gw), NEG_BIG, F32)

    def first_logits(j, band, c):
        return jnp.maximum(c, logits(0, j, band))
    m8 = _quad_loop(0, far_steps, lambda j, c: first_logits(j, None, c), neg)
    m8 = band_steps(first_logits, m8)
    for g in range(ATT_KV_HEADS):
        m = jnp.max(m8, axis=0, keepdims=True)
        acc_scr[...] = jnp.zeros_like(acc_scr)
        if g + 1 < ATT_KV_HEADS:
            def both(j, band, m8n):
                exp_pv(g, j, m, 0)
                return jnp.maximum(m8n, logits(g + 1, j, band))
            m8 = _quad_loop(0, far_steps, lambda j, c: both(j, None, c), neg)
            m8 = band_steps(both, m8)
        else:
            _quad_loop(0, n_steps, lambda j, c: exp_pv(g, j, m, c), 0)
        finish(g)

    o_ref[0] = out_scr[...].T.astype(o_ref.dtype)


def _dsa_mixer(h3, gain, w_in, rel_bias):
    B, L, D = h3.shape
    top_k = min(TOPK_MAX, L // 4)
    assert L % KEY_STEP == 0 and top_k % KEY_BLOCK == 0
    tm = 512
    nq = ATT_HEADS * ATT_HEAD_DIM
    nkv = ATT_KV_HEADS * ATT_HEAD_DIM
    nqi = IDX_HEADS * IDX_HEAD_DIM
    nvt = ATT_KV_HEADS * ATT_V_ROWS
    width = w_in.shape[1]
    padded = nq + 2 * nkv + nqi + 128
    w = jnp.pad(w_in, ((0, 0), (0, padded - width))).astype(BF16)
    q, k, vt, qi, ki, wt = pl.pallas_call(
        _dsa_proj_kernel,
        grid=(B, L // tm),
        in_specs=[
            pl.BlockSpec((1, tm, D), lambda b, r: (b, r, 0)),
            pl.BlockSpec((1, D), lambda b, r: (0, 0)),
            pl.BlockSpec((D, padded), lambda b, r: (0, 0)),
        ],
        out_specs=[
            pl.BlockSpec((1, tm, nq), lambda b, r: (b, r, 0)),
            pl.BlockSpec((1, tm, nkv), lambda b, r: (b, r, 0)),
            pl.BlockSpec((1, nvt, tm), lambda b, r: (b, 0, r)),
            pl.BlockSpec((1, tm, nqi), lambda b, r: (b, r, 0)),
            pl.BlockSpec((1, tm, IDX_HEAD_DIM), lambda b, r: (b, r, 0)),
            pl.BlockSpec((1, IDX_HEADS, tm), lambda b, r: (b, 0, r)),
        ],
        out_shape=[
            jax.ShapeDtypeStruct((B, L, nq), BF16),
            jax.ShapeDtypeStruct((B, L, nkv), BF16),
            jax.ShapeDtypeStruct((B, nvt, L), BF16),
            jax.ShapeDtypeStruct((B, L, nqi), BF16),
            jax.ShapeDtypeStruct((B, L, IDX_HEAD_DIM), BF16),
            jax.ShapeDtypeStruct((B, IDX_HEADS, L), F32),
        ],
        compiler_params=_params(2),
        name="dsa_proj",
    )(h3, gain.reshape(1, D), w)

    bias = pl.pallas_call(
        _bias_kernel,
        in_specs=[pl.BlockSpec(memory_space=pltpu.SMEM)],
        out_specs=pl.BlockSpec(memory_space=pltpu.VMEM),
        out_shape=jax.ShapeDtypeStruct((ATT_KV_HEADS, BIAS_TILES, KEY_BLOCK, ATT_GROUP * KEY_BLOCK), F32),
        name="dsa_bias_tiles",
    )(rel_bias)

    tq = Q_BLOCK
    return pl.pallas_call(
        functools.partial(_dsa_attn_kernel, top_k=top_k, seq=L),
        grid=(B, L // tq),
        in_specs=[
            pl.BlockSpec((1, tq, nq), lambda b, i: (b, i, 0)),
            pl.BlockSpec((1, L, nkv), lambda b, i: (b, 0, 0)),
            pl.BlockSpec((1, nvt, L), lambda b, i: (b, 0, 0)),
            pl.BlockSpec((1, tq, nqi), lambda b, i: (b, i, 0)),
            pl.BlockSpec((1, L, IDX_HEAD_DIM), lambda b, i: (b, 0, 0)),
            pl.BlockSpec((1, IDX_HEADS, tq), lambda b, i: (b, 0, i)),
            pl.BlockSpec((ATT_KV_HEADS, BIAS_TILES, KEY_BLOCK, ATT_GROUP * KEY_BLOCK), lambda b, i: (0, 0, 0, 0)),
        ],
        out_specs=pl.BlockSpec((1, tq, D), lambda b, i: (b, i, 0)),
        out_shape=jax.ShapeDtypeStruct((B, L, D), BF16),
        scratch_shapes=[
            pltpu.VMEM((ATT_HEADS * tq, ATT_HEAD_DIM), BF16),
            pltpu.VMEM((IDX_HEADS * tq, IDX_HEAD_DIM), BF16),
            pltpu.VMEM((L, tq), I32),
            pltpu.VMEM((L, tq), I16),
            pltpu.VMEM((L, tq), F32),
            pltpu.VMEM((L, ATT_GROUP * tq), F32),
            pltpu.VMEM((L, ATT_GROUP * tq), F32),
            pltpu.VMEM((ATT_V_ROWS, ATT_GROUP * tq), F32),
            pltpu.VMEM((D, tq), F32),
            pltpu.VMEM((8, tq), I32),
        ],
        compiler_params=_params(2),
        name="dsa_attention",
    )(q, k, vt, qi, ki, wt, bias)


def _ffn_kernel(h_ref, o_ref, wo_ref, gain_ref, wup_ref, cw_ref, cb_ref, wdn_ref, fgain_ref, out_ref,
                xn_scr, hbuf, carry_scr, acc_scr, act_scr, *, tiles_per_seq, final):
    tm = h_ref.shape[0]
    d_ff = wdn_ref.shape[0]
    tn = FFN_TILE
    n_ff = d_ff // tn
    i = pl.program_id(0)

    hnew = h_ref[...] + _dot(o_ref[...], wo_ref[...])
    acc_scr[...] = hnew
    xn_scr[...] = _rms_norm(hnew, gain_ref[...]).astype(BF16)
    first = (i % tiles_per_seq) == 0

    @pl.when(i == 0)
    def _():
        carry_scr[...] = jnp.zeros_like(carry_scr)

    def halves(j):
        return ((0, j * tn), (tn, d_ff + j * tn))

    def up(j, slot):
        for l0, c0 in halves(j):
            hu = _dot(xn_scr[...], wup_ref[:, c0:c0 + tn])
            hbuf[slot, 0:8, l0:l0 + tn] = jnp.where(first, 0.0, carry_scr[:, c0:c0 + tn])
            hbuf[slot, 8:8 + tm, l0:l0 + tn] = hu
            carry_scr[:, c0:c0 + tn] = hu[tm - 8:tm, :]

    def down(j, slot):
        ys = []
        for l0, c0 in halves(j):
            cw = cw_ref[:, c0:c0 + tn]
            ys.append(cw[2:3, :] * hbuf[slot, 8:8 + tm, l0:l0 + tn] + cw[1:2, :] * hbuf[slot, 7:7 + tm, l0:l0 + tn]
                      + cw[0:1, :] * hbuf[slot, 6:6 + tm, l0:l0 + tn] + cb_ref[:, c0:c0 + tn])
        gate, upv = ys
        act_scr[:, j * tn:(j + 1) * tn] = (gate * jax.nn.sigmoid(gate) * upv).astype(BF16)

    up(0, 0)
    for j in range(n_ff):
        if j + 1 < n_ff:
            up(j + 1, (j + 1) % 2)
        down(j, j % 2)

    res = acc_scr[...] + _dot(act_scr[...], wdn_ref[...])
    if final:
        res = _rms_norm(res, fgain_ref[...])
    out_ref[...] = res


def _ffn(h2, o2, w_out, gain, w_up, conv_w, conv_b, w_down, final_gain, *, seq, final):
    M, D = h2.shape
    F = w_down.shape[0]
    tm = FFN_ROWS
    tn = FFN_TILE
    assert F % tn == 0 and seq % tm == 0
    const2 = lambda i: (0, 0)
    resident = dict(pipeline_mode=pl.Buffered(1))
    return pl.pallas_call(
        functools.partial(_ffn_kernel, tiles_per_seq=seq // tm, final=final),
        grid=(M // tm,),
        in_specs=[
            pl.BlockSpec((tm, D), lambda i: (i, 0)),
            pl.BlockSpec((tm, D), lambda i: (i, 0)),
            pl.BlockSpec((D, D), const2, **resident),
            pl.BlockSpec((1, D), const2),
            pl.BlockSpec((D, 2 * F), const2, **resident),
            pl.BlockSpec((3, 2 * F), const2),
            pl.BlockSpec((1, 2 * F), const2),
            pl.BlockSpec((F, D), const2, **resident),
            pl.BlockSpec((1, D), const2),
        ],
        out_specs=pl.BlockSpec((tm, D), lambda i: (i, 0)),
        out_shape=jax.ShapeDtypeStruct((M, D), F32),
        scratch_shapes=[
            pltpu.VMEM((tm, D), BF16),
            pltpu.VMEM((2, tm + 8, 2 * tn), F32),
            pltpu.VMEM((8, 2 * F), F32),
            pltpu.VMEM((tm, D), F32),
            pltpu.VMEM((tm, F), BF16),
        ],
        compiler_params=_params(1),
        name="outproj_convffn",
    )(h2, o2, w_out.astype(BF16), gain.reshape(1, D), w_up.astype(BF16), conv_w, conv_b.reshape(1, 2 * F),
      w_down.astype(BF16), final_gain.reshape(1, D))


def kernel(x, attn_norm, ffn_norm, hgrn_w_in, hgrn_w_out, hgrn_gate_norm, hgrn_lower_bounds, dsa_w_in, dsa_w_out, rel_bias, ffn_w_up, ffn_conv_w, ffn_conv_b, ffn_w_down, final_norm):
    B, L, D = x.shape
    depth = attn_norm.shape[0]
    h = x.reshape(B * L, D)
    for layer in range(depth):
        j = layer // 2
        h3 = h.reshape(B, L, D)
        if layer % 2 == 0:
            o = _hgrn_mixer(h3, attn_norm[layer], hgrn_w_in[j], hgrn_lower_bounds, hgrn_gate_norm[j], j)
            w_out = hgrn_w_out[j]
        else:
            o = _dsa_mixer(h3, attn_norm[layer], dsa_w_in[j], rel_bias)
            w_out = dsa_w_out[j]
        h = _ffn(h, o.reshape(B * L, D), w_out, ffn_norm[layer], ffn_w_up[layer], ffn_conv_w[layer],
                 ffn_conv_b[layer], ffn_w_down[layer], final_norm, seq=L, final=(layer == depth - 1))
    return h.reshape(B, L, D)
```

```python
import functools
import math

import jax
import jax.numpy as jnp
from jax import lax
from jax.experimental import pallas as pl
from jax.experimental.pallas import tpu as pltpu

F32 = jnp.float32
BF16 = jnp.bfloat16
I32 = jnp.int32
I16 = jnp.int16

EPS = 1e-6
NEG_BIG = -1e30
TINY = 1e-30

HGRN_HEAD_DIM = 128
HGRN_BLOCK = 256
HGRN_SUB = HGRN_BLOCK
HGRN_HEADS_PER_STEP = 4
ATT_HEADS = 16
ATT_HEAD_DIM = 64
ATT_KV_HEADS = 2
ATT_GROUP = ATT_HEADS // ATT_KV_HEADS
ATT_V_ROWS = ATT_HEAD_DIM + 16
IDX_HEADS = 8
IDX_HEAD_DIM = 64
TOPK_MAX = 256
REL_BUCKETS = 32
REL_MAX_DIST = 128
KEY_BLOCK = 128
Q_BLOCK = 2 * KEY_BLOCK
KEY_STEP = Q_BLOCK
BIAS_TILES = 2
FFN_ROWS = 512
FFN_TILE = 256
INT_MIN = -(2 ** 31)
LOG2E = 1.4426950408889634

VMEM_LIMIT_BYTES = 56 * 1024 * 1024


def _dot(a, b):
    return jnp.dot(a, b, preferred_element_type=F32)


def _dot_nt(a, b):
    return lax.dot_general(a, b, (((1,), (1,)), ((), ())), preferred_element_type=F32)


def _rms_norm(x, gain):
    y = x * lax.rsqrt(jnp.mean(x * x, axis=-1, keepdims=True) + EPS)
    return y * gain


def _pair_loop(lo, hi, body, init):
    def two(p, c):
        j = lo + 2 * p
        return body(j + 1, body(j, c))
    c = lax.fori_loop(0, (hi - lo) // 2, two, init)
    return lax.cond((hi - lo) % 2 == 1, lambda c: body(hi - 1, c), lambda c: c, c)


def _quad_loop(lo, hi, body, init):
    n = hi - lo

    def run(start, width, c):
        for d in range(width):
            c = body(start + d, c)
        return c
    c = lax.fori_loop(0, n // 4, lambda p, c: run(lo + 4 * p, 4, c), init)
    done = lo + (n // 4) * 4
    c = lax.cond((n & 2) != 0, lambda c: run(done, 2, c), lambda c: c, c)
    return lax.cond((n & 1) != 0, lambda c: body(hi - 1, c), lambda c: c, c)


def _params(n_axes):
    return pltpu.CompilerParams(dimension_semantics=("arbitrary",) * n_axes,
                                vmem_limit_bytes=VMEM_LIMIT_BYTES)


def _block_rows(a, m):
    c = a.shape[0]
    return a.reshape(c // m, m, a.shape[1])[:, m - 1:m, :]


def _spread_rows(r, m):
    n = r.shape[0]
    return jnp.broadcast_to(r, (n, m, r.shape[2])).reshape(n * m, r.shape[2])


def _hgrn_kernel(x_ref, gain_ref, wq_ref, wf_ref, wi_ref, wg_ref, lbraw_ref, gn_ref, o_ref,
                 xn_scr, proj_scr, st_scr, *, layer_j, seq):
    C = HGRN_BLOCK
    dh = HGRN_HEAD_DIM
    pw = HGRN_HEADS_PER_STEP * dh
    head = pl.program_id(1)

    @pl.when(head == 0)
    def _():
        rb = 256

        def norm_rows(r, carry):
            rows = pl.ds(pl.multiple_of(r * rb, rb), rb)
            xn_scr[rows, :] = _rms_norm(x_ref[0, rows, :], gain_ref[...]).astype(BF16)
            return carry
        lax.fori_loop(0, seq // rb, norm_rows, 0)

    pb = 512

    def proj_rows(r, carry):
        rows = pl.ds(pl.multiple_of(r * pb, pb), pb)
        xr = xn_scr[rows, :]
        for part, w_ref in enumerate((wq_ref, wf_ref, wi_ref, wg_ref)):
            proj_scr[rows, part * pw:(part + 1) * pw] = _dot(xr, w_ref[...])
        return carry
    lax.fori_loop(0, seq // pb, proj_rows, 0)

    raw = lbraw_ref[...]
    ex = jnp.exp(raw - jnp.max(raw, axis=0, keepdims=True))
    soft = ex / jnp.sum(ex, axis=0, keepdims=True)
    lb_all = jnp.zeros((1, raw.shape[1]), F32)
    for l in range(1, layer_j + 1):
        lb_all = lb_all + soft[l:l + 1, :]
    gn = gn_ref[...]

    S = HGRN_SUB
    tril = (lax.broadcasted_iota(I32, (C, C), 1) <= lax.broadcasted_iota(I32, (C, C), 0)).astype(BF16)
    ri = lax.broadcasted_iota(I32, (S, S), 0)
    ci = lax.broadcasted_iota(I32, (S, S), 1)
    trow = lax.broadcasted_iota(I32, (S, dh), 0)
    sub = trow & 7

    def same_block(n):
        sh = int(math.log2(n))
        return ((ri >> sh) == (ci >> sh)).astype(F32)

    st_scr[...] = jnp.zeros_like(st_scr)

    def head_step(rows, hh):
        c0 = hh * dh
        q = proj_scr[rows, c0:c0 + dh]
        f = proj_scr[rows, pw + c0:pw + c0 + dh]
        iv = proj_scr[rows, 2 * pw + c0:2 * pw + c0 + dh]
        g = proj_scr[rows, 3 * pw + c0:3 * pw + c0 + dh]
        lb = lb_all[:, hh * dh:(hh + 1) * dh]
        oml = 1.0 - lb

        e = jnp.exp(-jnp.abs(f))
        r = 1.0 / (1.0 + e)
        er = e * r
        pos = f >= 0.0
        forget = lb + oml * jnp.where(pos, r, er)
        logf = jnp.log(jnp.maximum(forget, TINY)) * LOG2E
        k = oml * jnp.where(pos, er, r)

        hi = logf.astype(BF16)
        r1 = logf - hi.astype(F32)
        mid = r1.astype(BF16)
        lo = (r1 - mid.astype(F32)).astype(BF16)
        b = _dot(tril, hi) + _dot(tril, mid) + _dot(tril, lo)

        def sub_scores(q, k, b):
            b8 = b.reshape(S // 8, 8, dh)
            qs, ks = [], []
            for rho in range(8):
                bref = _spread_rows(b8[:, rho:rho + 1, :], 8)
                qs.append(jnp.where(sub >= rho, q * jnp.exp2(b - bref), 0.0).astype(BF16))
                ks.append(jnp.where(sub == rho, k, 0.0).astype(BF16))
            acc = _dot_nt(jnp.concatenate(qs, axis=1), jnp.concatenate(ks, axis=1))
            for m in [8 << lv for lv in range(int(math.log2(S // 8)))]:
                last = _block_rows(b, m)
                prev = jnp.concatenate([jnp.zeros((1, 1, dh), F32), last[:-1]], axis=0)
                upper = ((trow >> int(math.log2(m))) & 1) == 1
                qm = jnp.where(upper, q * jnp.exp2(b - _spread_rows(prev, m)), 0.0).astype(BF16)
                km = jnp.where(upper, 0.0, k * jnp.exp2(_spread_rows(last, m) - b)).astype(BF16)
                acc = _dot_nt(qm, km) + same_block(m) * acc
            return acc

        outs = []
        for n in range(C // S):
            lo_r, hi_r = n * S, (n + 1) * S
            blocks = [sub_scores(q[lo_r:hi_r], k[lo_r:hi_r], b[lo_r:hi_r])]
            if n > 0:
                edge = b[lo_r - 1:lo_r, :]
                qe = (q[lo_r:hi_r] * jnp.exp2(b[lo_r:hi_r] - edge)).astype(BF16)
                ke = (k[0:lo_r] * jnp.exp2(edge - b[0:lo_r])).astype(BF16)
                blocks.insert(0, _dot_nt(qe, ke))
            outs.append(_dot(jnp.concatenate(blocks, axis=1).astype(BF16), iv[0:hi_r].astype(BF16)))
        o_intra = jnp.concatenate(outs, axis=0)

        st = st_scr[hh]
        o = o_intra + _dot_nt((q * jnp.exp2(b)).astype(BF16), st.astype(BF16))
        b_last = b[C - 1:C, :]
        kh = (k * jnp.exp2(b_last - b)).astype(BF16)
        st_scr[hh] = st * jnp.exp2(b_last) + _dot(iv.T.astype(BF16), kh)

        o = o * lax.rsqrt(jnp.mean(o * o, axis=-1, keepdims=True) + EPS) * gn
        o = o * (g * jax.nn.sigmoid(g))
        o_ref[0, rows, hh * dh:(hh + 1) * dh] = o.astype(o_ref.dtype)

    def step(c, carry):
        rows = pl.ds(pl.multiple_of(c * C, C), C)
        for hh in range(HGRN_HEADS_PER_STEP):
            head_step(rows, hh)
        return carry

    lax.fori_loop(0, seq // C, step, 0)


def _hgrn_mixer(h3, gain, w_in, lower_raw, gate_norm, layer_j):
    B, L, D = h3.shape
    dh = HGRN_HEAD_DIM
    H = D // dh
    n_layers = lower_raw.shape[0]
    hp = HGRN_HEADS_PER_STEP
    assert H % hp == 0
    steps = H // hp
    w = w_in.astype(BF16)

    def part_spec(part):
        return pl.BlockSpec((D, hp * dh), lambda b, h: (0, part * steps + h))
    return pl.pallas_call(
        functools.partial(_hgrn_kernel, layer_j=layer_j, seq=L),
        grid=(B, steps),
        in_specs=[
            pl.BlockSpec((1, L, D), lambda b, h: (b, 0, 0)),
            pl.BlockSpec((1, D), lambda b, h: (0, 0)),
            part_spec(0), part_spec(1), part_spec(2), part_spec(3),
            pl.BlockSpec((n_layers, hp * dh), lambda b, h: (0, h)),
            pl.BlockSpec((1, dh), lambda b, h: (0, 0)),
        ],
        out_specs=pl.BlockSpec((1, L, hp * dh), lambda b, h: (b, 0, h)),
        out_shape=jax.ShapeDtypeStruct((B, L, D), BF16),
        scratch_shapes=[pltpu.VMEM((L, D), BF16), pltpu.VMEM((L, hp * 4 * dh), F32),
                        pltpu.VMEM((hp, dh, dh), F32)],
        compiler_params=_params(2),
        name="hgrn_mixer",
    )(h3, gain.reshape(1, D), w, w, w, w, lower_raw, gate_norm.reshape(1, dh))


def _dsa_proj_kernel(x_ref, gain_ref, w_ref, q_ref, k_ref, vt_ref, qi_ref, ki_ref, wt_ref):
    nq = ATT_HEADS * ATT_HEAD_DIM
    nkv = ATT_KV_HEADS * ATT_HEAD_DIM
    nqi = IDX_HEADS * IDX_HEAD_DIM
    xn = _rms_norm(x_ref[0], gain_ref[...]).astype(BF16)
    p = _dot(xn, w_ref[...])
    c0 = 0
    q_ref[0] = (p[:, c0:c0 + nq] * (ATT_HEAD_DIM ** -0.5 * LOG2E)).astype(BF16)
    c0 += nq
    k_ref[0] = p[:, c0:c0 + nkv].astype(BF16)
    c0 += nkv
    vt = p[:, c0:c0 + nkv].T
    tail = (lax.broadcasted_iota(I32, (ATT_V_ROWS - ATT_HEAD_DIM, vt.shape[1]), 0) == 0).astype(F32)
    vt_ref[0] = jnp.concatenate(
        [blk for g in range(ATT_KV_HEADS) for blk in (vt[g * ATT_HEAD_DIM:(g + 1) * ATT_HEAD_DIM, :], tail)],
        axis=0).astype(BF16)
    c0 += nkv
    qi_ref[0] = p[:, c0:c0 + nqi].astype(BF16)
    c0 += nqi
    ki_ref[0] = p[:, c0:c0 + IDX_HEAD_DIM].astype(BF16)
    tail = p[:, c0:c0 + 128].T
    wt_ref[0] = tail[IDX_HEAD_DIM:IDX_HEAD_DIM + IDX_HEADS, :] * (IDX_HEADS ** -0.5 * IDX_HEAD_DIM ** -0.5)


def _bias_kernel(rb_ref, out_ref):
    kb = KEY_BLOCK
    s = lax.broadcasted_iota(I32, (kb, kb), 0)
    t = lax.broadcasted_iota(I32, (kb, kb), 1)
    max_exact = REL_BUCKETS // 2
    for tile in range(2):
        n = jnp.maximum(t - s + kb * tile, 0)
        nf = jnp.maximum(n, max_exact).astype(F32)
        large = max_exact + (jnp.log(nf / max_exact) / math.log(REL_MAX_DIST / max_exact)
                             * (REL_BUCKETS - max_exact)).astype(I32)
        large = jnp.minimum(large, REL_BUCKETS - 1)
        bucket = jnp.where(n < max_exact, n, large)
        for h in range(ATT_HEADS):
            def pick(bk, acc):
                return jnp.where(bucket == bk, rb_ref[bk, h], acc)
            tile_h = lax.fori_loop(0, REL_BUCKETS, pick, jnp.zeros((kb, kb), F32))
            out_ref[h // ATT_GROUP, tile, :, (h % ATT_GROUP) * kb:(h % ATT_GROUP + 1) * kb] = (
                (tile_h - rb_ref[REL_BUCKETS - 1, h]) * LOG2E)


def _dsa_attn_kernel(q_ref, k_ref, vt_ref, qi_ref, ki_ref, wt_ref, bias_ref, o_ref,
                     qs_scr, qis_scr, key_scr, half_scr, mask_scr, s0_scr, s1_scr, acc_scr, out_scr, thr_scr,
                     *, top_k, seq):
    tq = Q_BLOCK
    kb = KEY_BLOCK
    sb = KEY_STEP
    nsub = tq // kb
    hd = ATT_HEAD_DIM
    sw = ATT_GROUP * kb
    gw = nsub * sw
    i = pl.program_id(1)
    n_steps = i + 1

    for g in range(ATT_KV_HEADS):
        for th in range(nsub):
            for hh in range(ATT_GROUP):
                r0 = g * gw + th * sw + hh * kb
                c0 = (g * ATT_GROUP + hh) * hd
                qs_scr[r0:r0 + kb, :] = q_ref[0, th * kb:(th + 1) * kb, c0:c0 + hd]
    for h in range(IDX_HEADS):
        qis_scr[h * tq:(h + 1) * tq, :] = qi_ref[0, :, h * IDX_HEAD_DIM:(h + 1) * IDX_HEAD_DIM]
    wt = wt_ref[0]

    s_iota = lax.broadcasted_iota(I32, (sb, tq), 0)
    t_glob = i * tq + lax.broadcasted_iota(I32, (sb, tq), 1)

    def step_rows(j):
        return pl.ds(pl.multiple_of(j * sb, sb), sb)

    def fold(x, op):
        return op(x.reshape(sb // 8, 8, x.shape[1]), axis=0)

    def idx_step(j, carry):
        rows = step_rows(j)
        d = _dot_nt(ki_ref[0, rows, :], qis_scr[...])
        sc = jnp.zeros((sb, tq), F32)
        for h in range(IDX_HEADS):
            sc = sc + wt[h:h + 1, :] * jnp.maximum(d[:, h * tq:(h + 1) * tq], 0.0)
        sc = jnp.where(j * sb + s_iota > t_glob, NEG_BIG, sc) + 0.0
        bits = pltpu.bitcast(sc, I32)
        key = jnp.where(bits < 0, bits ^ 0x7FFFFFFF, bits)
        key_scr[rows, :] = key
        half_scr[rows, :] = (key >> 16).astype(I16)
        return carry
    _quad_loop(0, n_steps, idx_step, 0)

    def count(pred):
        def body(j, cnt):
            hit = pred(key_scr[step_rows(j), :], j * sb + s_iota)
            return cnt + fold(jnp.where(hit, 1, 0), jnp.sum)
        return jnp.sum(_pair_loop(0, n_steps, body, jnp.zeros((8, tq), I32)), axis=0, keepdims=True)

    def count_half(preds):
        one, nil = jnp.ones((), I16), jnp.zeros((), I16)

        def body(j, cnts):
            v = half_scr[step_rows(j), :]
            out = []
            for pred, cnt in zip(preds, cnts):
                ones = jnp.where(pred(v), one, nil)
                for r in range(sb // 16):
                    cnt = cnt + ones[r * 16:(r + 1) * 16, :]
                out.append(cnt)
            return tuple(out)
        cnts = _pair_loop(0, n_steps, body, (jnp.zeros((16, tq), I16),) * len(preds))
        return [jnp.sum(c.astype(I32), axis=0, keepdims=True) for c in cnts]

    def at_least(cand):
        c16 = cand.astype(I16)
        return lambda v: v >= c16

    def kth_largest_half(target):
        c0, = count_half([at_least(jnp.zeros((1, tq), I32))])
        t0 = jnp.where(c0 >= target, 0, -(2 ** 15)).astype(I32)

        def bit_step(it, t_cur):
            cand = t_cur | jnp.left_shift(jnp.int32(1), 14 - it)
            cnt, = count_half([at_least(cand)])
            return jnp.where(cnt >= target, cand, t_cur)
        return lax.fori_loop(0, 15, bit_step, t0)

    thr_scr[0:1, :] = jnp.full((1, tq), INT_MIN, I32)
    thr_scr[1:2, :] = jnp.full((1, tq), 2 ** 31 - 1, I32)

    @pl.when((i + 1) * tq > top_k)
    def _():
        t_hi = kth_largest_half(top_k)
        n_above, = count_half([lambda v: v > t_hi.astype(I16)])
        need_lo = top_k - n_above

        def low_step(j, carry):
            rows = step_rows(j)
            key = key_scr[rows, :]
            low = (key & 0xFFFF) - 2 ** 15
            half_scr[rows, :] = jnp.where((key >> 16) == t_hi, low, -(2 ** 15)).astype(I16)
            return carry
        _pair_loop(0, n_steps, low_step, 0)
        t_lo = kth_largest_half(need_lo)
        thr = (t_hi << 16) | (t_lo + 2 ** 15)
        thr_scr[0:1, :] = thr

        n_ge = count(lambda key, s: key >= thr)

        @pl.when(jnp.max(n_ge) > top_k)
        def _():
            need = top_k - count(lambda key, s: key > thr)

            pos_bits = (seq - 1).bit_length()

            def pos_step(it, j_cur):
                cand = j_cur | jnp.left_shift(jnp.int32(1), pos_bits - 1 - it)
                cnt = count(lambda key, s: (key == thr) & (s < cand))
                return jnp.where(cnt < need, cand, j_cur)
            thr_scr[1:2, :] = lax.fori_loop(0, pos_bits, pos_step, jnp.zeros((1, tq), I32))

    thr = thr_scr[0:1, :]
    j_last = thr_scr[1:2, :]

    def selection_mask(j):
        rows = step_rows(j)
        key = key_scr[rows, :]
        s_glob = j * sb + s_iota
        sel = ((key > thr) | ((key == thr) & (s_glob <= j_last))) & (s_glob <= t_glob)
        mk = jnp.where(sel, 0.0, NEG_BIG)
        mask_scr[rows, :] = mk
        return mk

    s_bufs = (s0_scr, s1_scr)
    far_steps = jnp.maximum(i - 1, 0)
    band_tiles = {"diag": ((0, 1), (None, 0)), "prev": ((None, None), (1, None)), None: None}

    def logits(g, j, band):
        tiles = band_tiles[band]
        rows = step_rows(j)
        s = _dot_nt(k_ref[0, rows, g * hd:(g + 1) * hd], qs_scr[g * gw:(g + 1) * gw, :])
        mk = selection_mask(j) if g == 0 else mask_scr[rows, :]
        s = s + jnp.concatenate([mk[:, th * kb:(th + 1) * kb] for th in range(nsub) for _ in range(ATT_GROUP)],
                                axis=1)
        if tiles is not None:
            def piece(sh, th):
                blk = s[sh * kb:(sh + 1) * kb, th * sw:(th + 1) * sw]
                return blk if tiles[sh][th] is None else blk + bias_ref[g, tiles[sh][th]]
            s = jnp.concatenate([jnp.concatenate([piece(sh, th) for th in range(nsub)], axis=1)
                                 for sh in range(nsub)], axis=0)
        s_bufs[g % 2][rows, :] = s
        return fold(s, jnp.max)

    def exp_pv(g, j, m, carry):
        rows = step_rows(j)
        p = jnp.exp2(s_bufs[g % 2][rows, :] - m)
        acc_scr[...] += _dot(vt_ref[0, g * ATT_V_ROWS:(g + 1) * ATT_V_ROWS, rows], p.astype(BF16))
        return carry

    def finish(g):
        res = acc_scr[0:hd, :] / acc_scr[hd:hd + 1, :]
        for th in range(nsub):
            for hh in range(ATT_GROUP):
                h = g * ATT_GROUP + hh
                out_scr[h * hd:(h + 1) * hd, th * kb:(th + 1) * kb] = res[:, th * sw + hh * kb:th * sw + (hh + 1) * kb]

    def band_steps(step, carry):
        return lax.cond(i >= 1, lambda c: step(i, "diag", step(i - 1, "prev", c)),
                        lambda c: step(i, "diag", c), carry)

    neg = jnp.full((8, gw), NEG_BIG, F32)

    def first_logits(j, band, c):
        return jnp.maximum(c, logits(0, j, band))
    m8 = _quad_loop(0, far_steps, lambda j, c: first_logits(j, None, c), neg)
    m8 = band_steps(first_logits, m8)
    for g in range(ATT_KV_HEADS):
        m = jnp.max(m8, axis=0, keepdims=True)
        acc_scr[...] = jnp.zeros_like(acc_scr)
        if g + 1 < ATT_KV_HEADS:
            def both(j, band, m8n):
                exp_pv(g, j, m, 0)
                return jnp.maximum(m8n, logits(g + 1, j, band))
            m8 = _quad_loop(0, far_steps, lambda j, c: both(j, None, c), neg)
            m8 = band_steps(both, m8)
        else:
            _quad_loop(0, n_steps, lambda j, c: exp_pv(g, j, m, c), 0)
        finish(g)

    o_ref[0] = out_scr[...].T.astype(o_ref.dtype)


def _dsa_mixer(h3, gain, w_in, rel_bias):
    B, L, D = h3.shape
    top_k = min(TOPK_MAX, L // 4)
    assert L % KEY_STEP == 0 and top_k % KEY_BLOCK == 0
    tm = 512
    nq = ATT_HEADS * ATT_HEAD_DIM
    nkv = ATT_KV_HEADS * ATT_HEAD_DIM
    nqi = IDX_HEADS * IDX_HEAD_DIM
    nvt = ATT_KV_HEADS * ATT_V_ROWS
    width = w_in.shape[1]
    padded = nq + 2 * nkv + nqi + 128
    w = jnp.pad(w_in, ((0, 0), (0, padded - width))).astype(BF16)
    q, k, vt, qi, ki, wt = pl.pallas_call(
        _dsa_proj_kernel,
        grid=(B, L // tm),
        in_specs=[
            pl.BlockSpec((1, tm, D), lambda b, r: (b, r, 0)),
            pl.BlockSpec((1, D), lambda b, r: (0, 0)),
            pl.BlockSpec((D, padded), lambda b, r: (0, 0)),
        ],
        out_specs=[
            pl.BlockSpec((1, tm, nq), lambda b, r: (b, r, 0)),
            pl.BlockSpec((1, tm, nkv), lambda b, r: (b, r, 0)),
            pl.BlockSpec((1, nvt, tm), lambda b, r: (b, 0, r)),
            pl.BlockSpec((1, tm, nqi), lambda b, r: (b, r, 0)),
            pl.BlockSpec((1, tm, IDX_HEAD_DIM), lambda b, r: (b, r, 0)),
            pl.BlockSpec((1, IDX_HEADS, tm), lambda b, r: (b, 0, r)),
        ],
        out_shape=[
            jax.ShapeDtypeStruct((B, L, nq), BF16),
            jax.ShapeDtypeStruct((B, L, nkv), BF16),
            jax.ShapeDtypeStruct((B, nvt, L), BF16),
            jax.ShapeDtypeStruct((B, L, nqi), BF16),
            jax.ShapeDtypeStruct((B, L, IDX_HEAD_DIM), BF16),
            jax.ShapeDtypeStruct((B, IDX_HEADS, L), F32),
        ],
        compiler_params=_params(2),
        name="dsa_proj",
    )(h3, gain.reshape(1, D), w)

    bias = pl.pallas_call(
        _bias_kernel,
        in_specs=[pl.BlockSpec(memory_space=pltpu.SMEM)],
        out_specs=pl.BlockSpec(memory_space=pltpu.VMEM),
        out_shape=jax.ShapeDtypeStruct((ATT_KV_HEADS, BIAS_TILES, KEY_BLOCK, ATT_GROUP * KEY_BLOCK), F32),
        name="dsa_bias_tiles",
    )(rel_bias)

    tq = Q_BLOCK
    return pl.pallas_call(
        functools.partial(_dsa_attn_kernel, top_k=top_k, seq=L),
        grid=(B, L // tq),
        in_specs=[
            pl.BlockSpec((1, tq, nq), lambda b, i: (b, i, 0)),
            pl.BlockSpec((1, L, nkv), lambda b, i: (b, 0, 0)),
            pl.BlockSpec((1, nvt, L), lambda b, i: (b, 0, 0)),
            pl.BlockSpec((1, tq, nqi), lambda b, i: (b, i, 0)),
            pl.BlockSpec((1, L, IDX_HEAD_DIM), lambda b, i: (b, 0, 0)),
            pl.BlockSpec((1, IDX_HEADS, tq), lambda b, i: (b, 0, i)),
            pl.BlockSpec((ATT_KV_HEADS, BIAS_TILES, KEY_BLOCK, ATT_GROUP * KEY_BLOCK), lambda b, i: (0, 0, 0, 0)),
        ],
        out_specs=pl.BlockSpec((1, tq, D), lambda b, i: (b, i, 0)),
        out_shape=jax.ShapeDtypeStruct((B, L, D), BF16),
        scratch_shapes=[
            pltpu.VMEM((ATT_HEADS * tq, ATT_HEAD_DIM), BF16),
            pltpu.VMEM((IDX_HEADS * tq, IDX_HEAD_DIM), BF16),
            pltpu.VMEM((L, tq), I32),
            pltpu.VMEM((L, tq), I16),
            pltpu.VMEM((L, tq), F32),
            pltpu.VMEM((L, ATT_GROUP * tq), F32),
            pltpu.VMEM((L, ATT_GROUP * tq), F32),
            pltpu.VMEM((ATT_V_ROWS, ATT_GROUP * tq), F32),
            pltpu.VMEM((D, tq), F32),
            pltpu.VMEM((8, tq), I32),
        ],
        compiler_params=_params(2),
        name="dsa_attention",
    )(q, k, vt, qi, ki, wt, bias)


def _ffn_kernel(h_ref, o_ref, wo_ref, gain_ref, wup_ref, cw_ref, cb_ref, wdn_ref, fgain_ref, out_ref,
                xn_scr, hbuf, carry_scr, acc_scr, act_scr, *, tiles_per_seq, final):
    tm = h_ref.shape[0]
    d_ff = wdn_ref.shape[0]
    tn = FFN_TILE
    n_ff = d_ff // tn
    i = pl.program_id(0)

    hnew = h_ref[...] + _dot(o_ref[...], wo_ref[...])
    acc_scr[...] = hnew
    xn_scr[...] = _rms_norm(hnew, gain_ref[...]).astype(BF16)
    first = (i % tiles_per_seq) == 0

    @pl.when(i == 0)
    def _():
        carry_scr[...] = jnp.zeros_like(carry_scr)

    def halves(j):
        return ((0, j * tn), (tn, d_ff + j * tn))

    def up(j, slot):
        for l0, c0 in halves(j):
            hu = _dot(xn_scr[...], wup_ref[:, c0:c0 + tn])
            hbuf[slot, 0:8, l0:l0 + tn] = jnp.where(first, 0.0, carry_scr[:, c0:c0 + tn])
            hbuf[slot, 8:8 + tm, l0:l0 + tn] = hu
            carry_scr[:, c0:c0 + tn] = hu[tm - 8:tm, :]

    def down(j, slot):
        ys = []
        for l0, c0 in halves(j):
            cw = cw_ref[:, c0:c0 + tn]
            ys.append(cw[2:3, :] * hbuf[slot, 8:8 + tm, l0:l0 + tn] + cw[1:2, :] * hbuf[slot, 7:7 + tm, l0:l0 + tn]
                      + cw[0:1, :] * hbuf[slot, 6:6 + tm, l0:l0 + tn] + cb_ref[:, c0:c0 + tn])
        gate, upv = ys
        act_scr[:, j * tn:(j + 1) * tn] = (gate * jax.nn.sigmoid(gate) * upv).astype(BF16)

    up(0, 0)
    for j in range(n_ff):
        if j + 1 < n_ff:
            up(j + 1, (j + 1) % 2)
        down(j, j % 2)

    res = acc_scr[...] + _dot(act_scr[...], wdn_ref[...])
    if final:
        res = _rms_norm(res, fgain_ref[...])
    out_ref[...] = res


def _ffn(h2, o2, w_out, gain, w_up, conv_w, conv_b, w_down, final_gain, *, seq, final):
    M, D = h2.shape
    F = w_down.shape[0]
    tm = FFN_ROWS
    tn = FFN_TILE
    assert F % tn == 0 and seq % tm == 0
    const2 = lambda i: (0, 0)
    resident = dict(pipeline_mode=pl.Buffered(1))
    return pl.pallas_call(
        functools.partial(_ffn_kernel, tiles_per_seq=seq // tm, final=final),
        grid=(M // tm,),
        in_specs=[
            pl.BlockSpec((tm, D), lambda i: (i, 0)),
            pl.BlockSpec((tm, D), lambda i: (i, 0)),
            pl.BlockSpec((D, D), const2, **resident),
            pl.BlockSpec((1, D), const2),
            pl.BlockSpec((D, 2 * F), const2, **resident),
            pl.BlockSpec((3, 2 * F), const2),
            pl.BlockSpec((1, 2 * F), const2),
            pl.BlockSpec((F, D), const2, **resident),
            pl.BlockSpec((1, D), const2),
        ],
        out_specs=pl.BlockSpec((tm, D), lambda i: (i, 0)),
        out_shape=jax.ShapeDtypeStruct((M, D), F32),
        scratch_shapes=[
            pltpu.VMEM((tm, D), BF16),
            pltpu.VMEM((2, tm + 8, 2 * tn), F32),
            pltpu.VMEM((8, 2 * F), F32),
            pltpu.VMEM((tm, D), F32),
            pltpu.VMEM((tm, F), BF16),
        ],
        compiler_params=_params(1),
        name="outproj_convffn",
    )(h2, o2, w_out.astype(BF16), gain.reshape(1, D), w_up.astype(BF16), conv_w, conv_b.reshape(1, 2 * F),
      w_down.astype(BF16), final_gain.reshape(1, D))


def kernel(x, attn_norm, ffn_norm, hgrn_w_in, hgrn_w_out, hgrn_gate_norm, hgrn_lower_bounds, dsa_w_in, dsa_w_out, rel_bias, ffn_w_up, ffn_conv_w, ffn_conv_b, ffn_w_down, final_norm):
    B, L, D = x.shape
    depth = attn_norm.shape[0]
    h = x.reshape(B * L, D)
    for layer in range(depth):
        j = layer // 2
        h3 = h.reshape(B, L, D)
        if layer % 2 == 0:
            o = _hgrn_mixer(h3, attn_norm[layer], hgrn_w_in[j], hgrn_lower_bounds, hgrn_gate_norm[j], j)
            w_out = hgrn_w_out[j]
        else:
            o = _dsa_mixer(h3, attn_norm[layer], dsa_w_in[j], rel_bias)
            w_out = dsa_w_out[j]
        h = _ffn(h, o.reshape(B * L, D), w_out, ffn_norm[layer], ffn_w_up[layer], ffn_conv_w[layer],
                 ffn_conv_b[layer], ffn_w_down[layer], final_norm, seq=L, final=(layer == depth - 1))
    return h.reshape(B, L, D)
```

```python
import functools
import math

import jax
import jax.numpy as jnp
from jax import lax
from jax.experimental import pallas as pl
from jax.experimental.pallas import tpu as pltpu

F32 = jnp.float32
BF16 = jnp.bfloat16
I32 = jnp.int32
I16 = jnp.int16

EPS = 1e-6
NEG_BIG = -1e30
TINY = 1e-30

HGRN_HEAD_DIM = 128
HGRN_BLOCK = 256
HGRN_SUB = HGRN_BLOCK
HGRN_HEADS_PER_STEP = 4
ATT_HEADS = 16
ATT_HEAD_DIM = 64
ATT_KV_HEADS = 2
ATT_GROUP = ATT_HEADS // ATT_KV_HEADS
ATT_V_ROWS = ATT_HEAD_DIM + 16
IDX_HEADS = 8
IDX_HEAD_DIM = 64
TOPK_MAX = 256
REL_BUCKETS = 32
REL_MAX_DIST = 128
KEY_BLOCK = 128
Q_BLOCK = 2 * KEY_BLOCK
KEY_STEP = Q_BLOCK
BIAS_TILES = 2
FFN_ROWS = 512
FFN_TILE = 256
INT_MIN = -(2 ** 31)
LOG2E = 1.4426950408889634

VMEM_LIMIT_BYTES = 56 * 1024 * 1024


def _dot(a, b):
    return jnp.dot(a, b, preferred_element_type=F32)


def _dot_nt(a, b):
    return lax.dot_general(a, b, (((1,), (1,)), ((), ())), preferred_element_type=F32)


def _rms_norm(x, gain):
    y = x * lax.rsqrt(jnp.mean(x * x, axis=-1, keepdims=True) + EPS)
    return y * gain


def _pair_loop(lo, hi, body, init):
    def two(p, c):
        j = lo + 2 * p
        return body(j + 1, body(j, c))
    c = lax.fori_loop(0, (hi - lo) // 2, two, init)
    return lax.cond((hi - lo) % 2 == 1, lambda c: body(hi - 1, c), lambda c: c, c)


def _quad_loop(lo, hi, body, init):
    n = hi - lo

    def run(start, width, c):
        for d in range(width):
            c = body(start + d, c)
        return c
    c = lax.fori_loop(0, n // 4, lambda p, c: run(lo + 4 * p, 4, c), init)
    done = lo + (n // 4) * 4
    c = lax.cond((n & 2) != 0, lambda c: run(done, 2, c), lambda c: c, c)
    return lax.cond((n & 1) != 0, lambda c: body(hi - 1, c), lambda c: c, c)


def _params(n_axes):
    return pltpu.CompilerParams(dimension_semantics=("arbitrary",) * n_axes,
                                vmem_limit_bytes=VMEM_LIMIT_BYTES)


def _block_rows(a, m):
    c = a.shape[0]
    return a.reshape(c // m, m, a.shape[1])[:, m - 1:m, :]


def _spread_rows(r, m):
    n = r.shape[0]
    return jnp.broadcast_to(r, (n, m, r.shape[2])).reshape(n * m, r.shape[2])


def _hgrn_kernel(x_ref, gain_ref, wq_ref, wf_ref, wi_ref, wg_ref, lbraw_ref, gn_ref, o_ref,
                 xn_scr, proj_scr, st_scr, *, layer_j, seq):
    C = HGRN_BLOCK
    dh = HGRN_HEAD_DIM
    pw = HGRN_HEADS_PER_STEP * dh
    head = pl.program_id(1)

    @pl.when(head == 0)
    def _():
        rb = 256

        def norm_rows(r, carry):
            rows = pl.ds(pl.multiple_of(r * rb, rb), rb)
            xn_scr[rows, :] = _rms_norm(x_ref[0, rows, :], gain_ref[...]).astype(BF16)
            return carry
        lax.fori_loop(0, seq // rb, norm_rows, 0)

    pb = 512

    def proj_rows(r, carry):
        rows = pl.ds(pl.multiple_of(r * pb, pb), pb)
        xr = xn_scr[rows, :]
        for part, w_ref in enumerate((wq_ref, wf_ref, wi_ref, wg_ref)):
            proj_scr[rows, part * pw:(part + 1) * pw] = _dot(xr, w_ref[...])
        return carry
    lax.fori_loop(0, seq // pb, proj_rows, 0)

    raw = lbraw_ref[...]
    ex = jnp.exp(raw - jnp.max(raw, axis=0, keepdims=True))
    soft = ex / jnp.sum(ex, axis=0, keepdims=True)
    lb_all = jnp.zeros((1, raw.shape[1]), F32)
    for l in range(1, layer_j + 1):
        lb_all = lb_all + soft[l:l + 1, :]
    gn = gn_ref[...]

    S = HGRN_SUB
    tril = (lax.broadcasted_iota(I32, (C, C), 1) <= lax.broadcasted_iota(I32, (C, C), 0)).astype(BF16)
    ri = lax.broadcasted_iota(I32, (S, S), 0)
    ci = lax.broadcasted_iota(I32, (S, S), 1)
    trow = lax.broadcasted_iota(I32, (S, dh), 0)
    sub = trow & 7

    def same_block(n):
        sh = int(math.log2(n))
        return ((ri >> sh) == (ci >> sh)).astype(F32)

    st_scr[...] = jnp.zeros_like(st_scr)

    def head_step(rows, hh):
        c0 = hh * dh
        q = proj_scr[rows, c0:c0 + dh]
        f = proj_scr[rows, pw + c0:pw + c0 + dh]
        iv = proj_scr[rows, 2 * pw + c0:2 * pw + c0 + dh]
        g = proj_scr[rows, 3 * pw + c0:3 * pw + c0 + dh]
        lb = lb_all[:, hh * dh:(hh + 1) * dh]
        oml = 1.0 - lb

        e = jnp.exp(-jnp.abs(f))
        r = 1.0 / (1.0 + e)
        er = e * r
        pos = f >= 0.0
        forget = lb + oml * jnp.where(pos, r, er)
        logf = jnp.log(jnp.maximum(forget, TINY)) * LOG2E
        k = oml * jnp.where(pos, er, r)

        hi = logf.astype(BF16)
        r1 = logf - hi.astype(F32)
        mid = r1.astype(BF16)
        lo = (r1 - mid.astype(F32)).astype(BF16)
        b = _dot(tril, hi) + _dot(tril, mid) + _dot(tril, lo)

        def sub_scores(q, k, b):
            b8 = b.reshape(S // 8, 8, dh)
            qs, ks = [], []
            for rho in range(8):
                bref = _spread_rows(b8[:, rho:rho + 1, :], 8)
                qs.append(jnp.where(sub >= rho, q * jnp.exp2(b - bref), 0.0).astype(BF16))
                ks.append(jnp.where(sub == rho, k, 0.0).astype(BF16))
            acc = _dot_nt(jnp.concatenate(qs, axis=1), jnp.concatenate(ks, axis=1))
            for m in [8 << lv for lv in range(int(math.log2(S // 8)))]:
                last = _block_rows(b, m)
                prev = jnp.concatenate([jnp.zeros((1, 1, dh), F32), last[:-1]], axis=0)
                upper = ((trow >> int(math.log2(m))) & 1) == 1
                qm = jnp.where(upper, q * jnp.exp2(b - _spread_rows(prev, m)), 0.0).astype(BF16)
                km = jnp.where(upper, 0.0, k * jnp.exp2(_spread_rows(last, m) - b)).astype(BF16)
                acc = _dot_nt(qm, km) + same_block(m) * acc
            return acc

        outs = []
        for n in range(C // S):
            lo_r, hi_r = n * S, (n + 1) * S
            blocks = [sub_scores(q[lo_r:hi_r], k[lo_r:hi_r], b[lo_r:hi_r])]
            if n > 0:
                edge = b[lo_r - 1:lo_r, :]
                qe = (q[lo_r:hi_r] * jnp.exp2(b[lo_r:hi_r] - edge)).astype(BF16)
                ke = (k[0:lo_r] * jnp.exp2(edge - b[0:lo_r])).astype(BF16)
                blocks.insert(0, _dot_nt(qe, ke))
            outs.append(_dot(jnp.concatenate(blocks, axis=1).astype(BF16), iv[0:hi_r].astype(BF16)))
        o_intra = jnp.concatenate(outs, axis=0)

        st = st_scr[hh]
        o = o_intra + _dot_nt((q * jnp.exp2(b)).astype(BF16), st.astype(BF16))
        b_last = b[C - 1:C, :]
        kh = (k * jnp.exp2(b_last - b)).astype(BF16)
        st_scr[hh] = st * jnp.exp2(b_last) + _dot(iv.T.astype(BF16), kh)

        o = o * lax.rsqrt(jnp.mean(o * o, axis=-1, keepdims=True) + EPS) * gn
        o = o * (g * jax.nn.sigmoid(g))
        o_ref[0, rows, hh * dh:(hh + 1) * dh] = o.astype(o_ref.dtype)

    def step(c, carry):
        rows = pl.ds(pl.multiple_of(c * C, C), C)
        for hh in range(HGRN_HEADS_PER_STEP):
            head_step(rows, hh)
        return carry

    _pair_loop(0, seq // C, step, 0)


def _hgrn_mixer(h3, gain, w_in, lower_raw, gate_norm, layer_j):
    B, L, D = h3.shape
    dh = HGRN_HEAD_DIM
    H = D // dh
    n_layers = lower_raw.shape[0]
    hp = HGRN_HEADS_PER_STEP
    assert H % hp == 0
    steps = H // hp
    w = w_in.astype(BF16)

    def part_spec(part):
        return pl.BlockSpec((D, hp * dh), lambda b, h: (0, part * steps + h))
    return pl.pallas_call(
        functools.partial(_hgrn_kernel, layer_j=layer_j, seq=L),
        grid=(B, steps),
        in_specs=[
            pl.BlockSpec((1, L, D), lambda b, h: (b, 0, 0)),
            pl.BlockSpec((1, D), lambda b, h: (0, 0)),
            part_spec(0), part_spec(1), part_spec(2), part_spec(3),
            pl.BlockSpec((n_layers, hp * dh), lambda b, h: (0, h)),
            pl.BlockSpec((1, dh), lambda b, h: (0, 0)),
        ],
        out_specs=pl.BlockSpec((1, L, hp * dh), lambda b, h: (b, 0, h)),
        out_shape=jax.ShapeDtypeStruct((B, L, D), BF16),
        scratch_shapes=[pltpu.VMEM((L, D), BF16), pltpu.VMEM((L, hp * 4 * dh), F32),
                        pltpu.VMEM((hp, dh, dh), F32)],
        compiler_params=_params(2),
        name="hgrn_mixer",
    )(h3, gain.reshape(1, D), w, w, w, w, lower_raw, gate_norm.reshape(1, dh))


def _dsa_proj_kernel(x_ref, gain_ref, w_ref, q_ref, k_ref, vt_ref, qi_ref, ki_ref, wt_ref):
    nq = ATT_HEADS * ATT_HEAD_DIM
    nkv = ATT_KV_HEADS * ATT_HEAD_DIM
    nqi = IDX_HEADS * IDX_HEAD_DIM
    xn = _rms_norm(x_ref[0], gain_ref[...]).astype(BF16)
    p = _dot(xn, w_ref[...])
    c0 = 0
    q_ref[0] = (p[:, c0:c0 + nq] * (ATT_HEAD_DIM ** -0.5 * LOG2E)).astype(BF16)
    c0 += nq
    k_ref[0] = p[:, c0:c0 + nkv].astype(BF16)
    c0 += nkv
    vt = p[:, c0:c0 + nkv].T
    tail = (lax.broadcasted_iota(I32, (ATT_V_ROWS - ATT_HEAD_DIM, vt.shape[1]), 0) == 0).astype(F32)
    vt_ref[0] = jnp.concatenate(
        [blk for g in range(ATT_KV_HEADS) for blk in (vt[g * ATT_HEAD_DIM:(g + 1) * ATT_HEAD_DIM, :], tail)],
        axis=0).astype(BF16)
    c0 += nkv
    qi_ref[0] = p[:, c0:c0 + nqi].astype(BF16)
    c0 += nqi
    ki_ref[0] = p[:, c0:c0 + IDX_HEAD_DIM].astype(BF16)
    tail = p[:, c0:c0 + 128].T
    wt_ref[0] = tail[IDX_HEAD_DIM:IDX_HEAD_DIM + IDX_HEADS, :] * (IDX_HEADS ** -0.5 * IDX_HEAD_DIM ** -0.5)


def _bias_kernel(rb_ref, out_ref):
    kb = KEY_BLOCK
    s = lax.broadcasted_iota(I32, (kb, kb), 0)
    t = lax.broadcasted_iota(I32, (kb, kb), 1)
    max_exact = REL_BUCKETS // 2
    for tile in range(2):
        n = jnp.maximum(t - s + kb * tile, 0)
        nf = jnp.maximum(n, max_exact).astype(F32)
        large = max_exact + (jnp.log(nf / max_exact) / math.log(REL_MAX_DIST / max_exact)
                             * (REL_BUCKETS - max_exact)).astype(I32)
        large = jnp.minimum(large, REL_BUCKETS - 1)
        bucket = jnp.where(n < max_exact, n, large)
        for h in range(ATT_HEADS):
            def pick(bk, acc):
                return jnp.where(bucket == bk, rb_ref[bk, h], acc)
            tile_h = lax.fori_loop(0, REL_BUCKETS, pick, jnp.zeros((kb, kb), F32))
            out_ref[h // ATT_GROUP, tile, :, (h % ATT_GROUP) * kb:(h % ATT_GROUP + 1) * kb] = (
                (tile_h - rb_ref[REL_BUCKETS - 1, h]) * LOG2E)


def _dsa_attn_kernel(q_ref, k_ref, vt_ref, qi_ref, ki_ref, wt_ref, bias_ref, o_ref,
                     qs_scr, qis_scr, key_scr, half_scr, mask_scr, s0_scr, s1_scr, acc_scr, out_scr, thr_scr,
                     *, top_k, seq):
    tq = Q_BLOCK
    kb = KEY_BLOCK
    sb = KEY_STEP
    nsub = tq // kb
    hd = ATT_HEAD_DIM
    sw = ATT_GROUP * kb
    gw = nsub * sw
    i = pl.program_id(1)
    n_steps = i + 1

    for g in range(ATT_KV_HEADS):
        for th in range(nsub):
            for hh in range(ATT_GROUP):
                r0 = g * gw + th * sw + hh * kb
                c0 = (g * ATT_GROUP + hh) * hd
                qs_scr[r0:r0 + kb, :] = q_ref[0, th * kb:(th + 1) * kb, c0:c0 + hd]
    for h in range(IDX_HEADS):
        qis_scr[h * tq:(h + 1) * tq, :] = qi_ref[0, :, h * IDX_HEAD_DIM:(h + 1) * IDX_HEAD_DIM]
    wt = wt_ref[0]

    s_iota = lax.broadcasted_iota(I32, (sb, tq), 0)
    t_glob = i * tq + lax.broadcasted_iota(I32, (sb, tq), 1)

    def step_rows(j):
        return pl.ds(pl.multiple_of(j * sb, sb), sb)

    def fold(x, op):
        return op(x.reshape(sb // 8, 8, x.shape[1]), axis=0)

    def idx_step(j, carry):
        rows = step_rows(j)
        d = _dot_nt(ki_ref[0, rows, :], qis_scr[...])
        sc = jnp.zeros((sb, tq), F32)
        for h in range(IDX_HEADS):
            sc = sc + wt[h:h + 1, :] * jnp.maximum(d[:, h * tq:(h + 1) * tq], 0.0)
        sc = jnp.where(j * sb + s_iota > t_glob, NEG_BIG, sc) + 0.0
        bits = pltpu.bitcast(sc, I32)
        key = jnp.where(bits < 0, bits ^ 0x7FFFFFFF, bits)
        key_scr[rows, :] = key
        half_scr[rows, :] = (key >> 16).astype(I16)
        return carry
    _quad_loop(0, n_steps, idx_step, 0)

    def count(pred):
        def body(j, cnt):
            hit = pred(key_scr[step_rows(j), :], j * sb + s_iota)
            return cnt + fold(jnp.where(hit, 1, 0), jnp.sum)
        return jnp.sum(_pair_loop(0, n_steps, body, jnp.zeros((8, tq), I32)), axis=0, keepdims=True)

    def count_half(preds):
        one, nil = jnp.ones((), I16), jnp.zeros((), I16)

        def body(j, cnts):
            v = half_scr[step_rows(j), :]
            out = []
            for pred, cnt in zip(preds, cnts):
                ones = jnp.where(pred(v), one, nil)
                for r in range(sb // 16):
                    cnt = cnt + ones[r * 16:(r + 1) * 16, :]
                out.append(cnt)
            return tuple(out)
        cnts = _pair_loop(0, n_steps, body, (jnp.zeros((16, tq), I16),) * len(preds))
        return [jnp.sum(c.astype(I32), axis=0, keepdims=True) for c in cnts]

    def at_least(cand):
        c16 = cand.astype(I16)
        return lambda v: v >= c16

    def kth_largest_half(target):
        c0, = count_half([at_least(jnp.zeros((1, tq), I32))])
        t0 = jnp.where(c0 >= target, 0, -(2 ** 15)).astype(I32)

        def bit_step(it, t_cur):
            cand = t_cur | jnp.left_shift(jnp.int32(1), 14 - it)
            cnt, = count_half([at_least(cand)])
            return jnp.where(cnt >= target, cand, t_cur)
        return lax.fori_loop(0, 15, bit_step, t0)

    thr_scr[0:1, :] = jnp.full((1, tq), INT_MIN, I32)
    thr_scr[1:2, :] = jnp.full((1, tq), 2 ** 31 - 1, I32)

    @pl.when((i + 1) * tq > top_k)
    def _():
        t_hi = kth_largest_half(top_k)
        n_above, = count_half([lambda v: v > t_hi.astype(I16)])
        need_lo = top_k - n_above

        def low_step(j, carry):
            rows = step_rows(j)
            key = key_scr[rows, :]
            low = (key & 0xFFFF) - 2 ** 15
            half_scr[rows, :] = jnp.where((key >> 16) == t_hi, low, -(2 ** 15)).astype(I16)
            return carry
        _pair_loop(0, n_steps, low_step, 0)
        t_lo = kth_largest_half(need_lo)
        thr = (t_hi << 16) | (t_lo + 2 ** 15)
        thr_scr[0:1, :] = thr

        n_ge = count(lambda key, s: key >= thr)

        @pl.when(jnp.max(n_ge) > top_k)
        def _():
            need = top_k - count(lambda key, s: key > thr)

            pos_bits = (seq - 1).bit_length()

            def pos_step(it, j_cur):
                cand = j_cur | jnp.left_shift(jnp.int32(1), pos_bits - 1 - it)
                cnt = count(lambda key, s: (key == thr) & (s < cand))
                return jnp.where(cnt < need, cand, j_cur)
            thr_scr[1:2, :] = lax.fori_loop(0, pos_bits, pos_step, jnp.zeros((1, tq), I32))

    thr = thr_scr[0:1, :]
    j_last = thr_scr[1:2, :]

    def selection_mask(j):
        rows = step_rows(j)
        key = key_scr[rows, :]
        s_glob = j * sb + s_iota
        sel = ((key > thr) | ((key == thr) & (s_glob <= j_last))) & (s_glob <= t_glob)
        mk = jnp.where(sel, 0.0, NEG_BIG)
        mask_scr[rows, :] = mk
        return mk

    s_bufs = (s0_scr, s1_scr)
    far_steps = jnp.maximum(i - 1, 0)
    band_tiles = {"diag": ((0, 1), (None, 0)), "prev": ((None, None), (1, None)), None: None}

    def logits(g, j, band):
        tiles = band_tiles[band]
        rows = step_rows(j)
        s = _dot_nt(k_ref[0, rows, g * hd:(g + 1) * hd], qs_scr[g * gw:(g + 1) * gw, :])
        mk = selection_mask(j) if g == 0 else mask_scr[rows, :]
        s = s + jnp.concatenate([mk[:, th * kb:(th + 1) * kb] for th in range(nsub) for _ in range(ATT_GROUP)],
                                axis=1)
        if tiles is not None:
            def piece(sh, th):
                blk = s[sh * kb:(sh + 1) * kb, th * sw:(th + 1) * sw]
                return blk if tiles[sh][th] is None else blk + bias_ref[g, tiles[sh][th]]
            s = jnp.concatenate([jnp.concatenate([piece(sh, th) for th in range(nsub)], axis=1)
                                 for sh in range(nsub)], axis=0)
        s_bufs[g % 2][rows, :] = s
        return fold(s, jnp.max)

    def exp_pv(g, j, m, carry):
        rows = step_rows(j)
        p = jnp.exp2(s_bufs[g % 2][rows, :] - m)
        acc_scr[...] += _dot(vt_ref[0, g * ATT_V_ROWS:(g + 1) * ATT_V_ROWS, rows], p.astype(BF16))
        return carry

    def finish(g):
        res = acc_scr[0:hd, :] / acc_scr[hd:hd + 1, :]
        for th in range(nsub):
            for hh in range(ATT_GROUP):
                h = g * ATT_GROUP + hh
                out_scr[h * hd:(h + 1) * hd, th * kb:(th + 1) * kb] = res[:, th * sw + hh * kb:th * sw + (hh + 1) * kb]

    def band_steps(step, carry):
        return lax.cond(i >= 1, lambda c: step(i, "diag", step(i - 1, "prev", c)),
                        lambda c: step(i, "diag", c), carry)

    neg = jnp.full((8, gw), NEG_BIG, F32)

    def first_logits(j, band, c):
        return jnp.maximum(c, logits(0, j, band))
    m8 = _quad_loop(0, far_steps, lambda j, c: first_logits(j, None, c), neg)
    m8 = band_steps(first_logits, m8)
    for g in range(ATT_KV_HEADS):
        m = jnp.max(m8, axis=0, keepdims=True)
        acc_scr[...] = jnp.zeros_like(acc_scr)
        if g + 1 < ATT_KV_HEADS:
            def both(j, band, m8n):
                exp_pv(g, j, m, 0)
                return jnp.maximum(m8n, logits(g + 1, j, band))
            m8 = _quad_loop(0, far_steps, lambda j, c: both(j, None, c), neg)
            m8 = band_steps(both, m8)
        else:
            _quad_loop(0, n_steps, lambda j, c: exp_pv(g, j, m, c), 0)
        finish(g)

    o_ref[0] = out_scr[...].T.astype(o_ref.dtype)


def _dsa_mixer(h3, gain, w_in, rel_bias):
    B, L, D = h3.shape
    top_k = min(TOPK_MAX, L // 4)
    assert L % KEY_STEP == 0 and top_k % KEY_BLOCK == 0
    tm = 512
    nq = ATT_HEADS * ATT_HEAD_DIM
    nkv = ATT_KV_HEADS * ATT_HEAD_DIM
    nqi = IDX_HEADS * IDX_HEAD_DIM
    nvt = ATT_KV_HEADS * ATT_V_ROWS
    width = w_in.shape[1]
    padded = nq + 2 * nkv + nqi + 128
    w = jnp.pad(w_in, ((0, 0), (0, padded - width))).astype(BF16)
    q, k, vt, qi, ki, wt = pl.pallas_call(
        _dsa_proj_kernel,
        grid=(B, L // tm),
        in_specs=[
            pl.BlockSpec((1, tm, D), lambda b, r: (b, r, 0)),
            pl.BlockSpec((1, D), lambda b, r: (0, 0)),
            pl.BlockSpec((D, padded), lambda b, r: (0, 0)),
        ],
        out_specs=[
            pl.BlockSpec((1, tm, nq), lambda b, r: (b, r, 0)),
            pl.BlockSpec((1, tm, nkv), lambda b, r: (b, r, 0)),
            pl.BlockSpec((1, nvt, tm), lambda b, r: (b, 0, r)),
            pl.BlockSpec((1, tm, nqi), lambda b, r: (b, r, 0)),
            pl.BlockSpec((1, tm, IDX_HEAD_DIM), lambda b, r: (b, r, 0)),
            pl.BlockSpec((1, IDX_HEADS, tm), lambda b, r: (b, 0, r)),
        ],
        out_shape=[
            jax.ShapeDtypeStruct((B, L, nq), BF16),
            jax.ShapeDtypeStruct((B, L, nkv), BF16),
            jax.ShapeDtypeStruct((B, nvt, L), BF16),
            jax.ShapeDtypeStruct((B, L, nqi), BF16),
            jax.ShapeDtypeStruct((B, L, IDX_HEAD_DIM), BF16),
            jax.ShapeDtypeStruct((B, IDX_HEADS, L), F32),
        ],
        compiler_params=_params(2),
        name="dsa_proj",
    )(h3, gain.reshape(1, D), w)

    bias = pl.pallas_call(
        _bias_kernel,
        in_specs=[pl.BlockSpec(memory_space=pltpu.SMEM)],
        out_specs=pl.BlockSpec(memory_space=pltpu.VMEM),
        out_shape=jax.ShapeDtypeStruct((ATT_KV_HEADS, BIAS_TILES, KEY_BLOCK, ATT_GROUP * KEY_BLOCK), F32),
        name="dsa_bias_tiles",
    )(rel_bias)

    tq = Q_BLOCK
    return pl.pallas_call(
        functools.partial(_dsa_attn_kernel, top_k=top_k, seq=L),
        grid=(B, L // tq),
        in_specs=[
            pl.BlockSpec((1, tq, nq), lambda b, i: (b, i, 0)),
            pl.BlockSpec((1, L, nkv), lambda b, i: (b, 0, 0)),
            pl.BlockSpec((1, nvt, L), lambda b, i: (b, 0, 0)),
            pl.BlockSpec((1, tq, nqi), lambda b, i: (b, i, 0)),
            pl.BlockSpec((1, L, IDX_HEAD_DIM), lambda b, i: (b, 0, 0)),
            pl.BlockSpec((1, IDX_HEADS, tq), lambda b, i: (b, 0, i)),
            pl.BlockSpec((ATT_KV_HEADS, BIAS_TILES, KEY_BLOCK, ATT_GROUP * KEY_BLOCK), lambda b, i: (0, 0, 0, 0)),
        ],
        out_specs=pl.BlockSpec((1, tq, D), lambda b, i: (b, i, 0)),
        out_shape=jax.ShapeDtypeStruct((B, L, D), BF16),
        scratch_shapes=[
            pltpu.VMEM((ATT_HEADS * tq, ATT_HEAD_DIM), BF16),
            pltpu.VMEM((IDX_HEADS * tq, IDX_HEAD_DIM), BF16),
            pltpu.VMEM((L, tq), I32),
            pltpu.VMEM((L, tq), I16),
            pltpu.VMEM((L, tq), F32),
            pltpu.VMEM((L, ATT_GROUP * tq), F32),
            pltpu.VMEM((L, ATT_GROUP * tq), F32),
            pltpu.VMEM((ATT_V_ROWS, ATT_GROUP * tq), F32),
            pltpu.VMEM((D, tq), F32),
            pltpu.VMEM((8, tq), I32),
        ],
        compiler_params=_params(2),
        name="dsa_attention",
    )(q, k, vt, qi, ki, wt, bias)


def _ffn_kernel(h_ref, o_ref, wo_ref, gain_ref, wup_ref, cw_ref, cb_ref, wdn_ref, fgain_ref, out_ref,
                xn_scr, hbuf, carry_scr, acc_scr, act_scr, *, tiles_per_seq, final):
    tm = h_ref.shape[0]
    d_ff = wdn_ref.shape[0]
    tn = FFN_TILE
    n_ff = d_ff // tn
    i = pl.program_id(0)

    hnew = h_ref[...] + _dot(o_ref[...], wo_ref[...])
    acc_scr[...] = hnew
    xn_scr[...] = _rms_norm(hnew, gain_ref[...]).astype(BF16)
    first = (i % tiles_per_seq) == 0

    @pl.when(i == 0)
    def _():
        carry_scr[...] = jnp.zeros_like(carry_scr)

    def halves(j):
        return ((0, j * tn), (tn, d_ff + j * tn))

    def up(j, slot):
        for l0, c0 in halves(j):
            hu = _dot(xn_scr[...], wup_ref[:, c0:c0 + tn])
            hbuf[slot, 0:8, l0:l0 + tn] = jnp.where(first, 0.0, carry_scr[:, c0:c0 + tn])
            hbuf[slot, 8:8 + tm, l0:l0 + tn] = hu
            carry_scr[:, c0:c0 + tn] = hu[tm - 8:tm, :]

    def down(j, slot):
        ys = []
        for l0, c0 in halves(j):
            cw = cw_ref[:, c0:c0 + tn]
            ys.append(cw[2:3, :] * hbuf[slot, 8:8 + tm, l0:l0 + tn] + cw[1:2, :] * hbuf[slot, 7:7 + tm, l0:l0 + tn]
                      + cw[0:1, :] * hbuf[slot, 6:6 + tm, l0:l0 + tn] + cb_ref[:, c0:c0 + tn])
        gate, upv = ys
        act_scr[:, j * tn:(j + 1) * tn] = (gate * jax.nn.sigmoid(gate) * upv).astype(BF16)

    up(0, 0)
    for j in range(n_ff):
        if j + 1 < n_ff:
            up(j + 1, (j + 1) % 2)
        down(j, j % 2)

    res = acc_scr[...] + _dot(act_scr[...], wdn_ref[...])
    if final:
        res = _rms_norm(res, fgain_ref[...])
    out_ref[...] = res


def _ffn(h2, o2, w_out, gain, w_up, conv_w, conv_b, w_down, final_gain, *, seq, final):
    M, D = h2.shape
    F = w_down.shape[0]
    tm = FFN_ROWS
    tn = FFN_TILE
    assert F % tn == 0 and seq % tm == 0
    const2 = lambda i: (0, 0)
    resident = dict(pipeline_mode=pl.Buffered(1))
    return pl.pallas_call(
        functools.partial(_ffn_kernel, tiles_per_seq=seq // tm, final=final),
        grid=(M // tm,),
        in_specs=[
            pl.BlockSpec((tm, D), lambda i: (i, 0)),
            pl.BlockSpec((tm, D), lambda i: (i, 0)),
            pl.BlockSpec((D, D), const2, **resident),
            pl.BlockSpec((1, D), const2),
            pl.BlockSpec((D, 2 * F), const2, **resident),
            pl.BlockSpec((3, 2 * F), const2),
            pl.BlockSpec((1, 2 * F), const2),
            pl.BlockSpec((F, D), const2, **resident),
            pl.BlockSpec((1, D), const2),
        ],
        out_specs=pl.BlockSpec((tm, D), lambda i: (i, 0)),
        out_shape=jax.ShapeDtypeStruct((M, D), F32),
        scratch_shapes=[
            pltpu.VMEM((tm, D), BF16),
            pltpu.VMEM((2, tm + 8, 2 * tn), F32),
            pltpu.VMEM((8, 2 * F), F32),
            pltpu.VMEM((tm, D), F32),
            pltpu.VMEM((tm, F), BF16),
        ],
        compiler_params=_params(1),
        name="outproj_convffn",
    )(h2, o2, w_out.astype(BF16), gain.reshape(1, D), w_up.astype(BF16), conv_w, conv_b.reshape(1, 2 * F),
      w_down.astype(BF16), final_gain.reshape(1, D))


def kernel(x, attn_norm, ffn_norm, hgrn_w_in, hgrn_w_out, hgrn_gate_norm, hgrn_lower_bounds, dsa_w_in, dsa_w_out, rel_bias, ffn_w_up, ffn_conv_w, ffn_conv_b, ffn_w_down, final_norm):
    B, L, D = x.shape
    depth = attn_norm.shape[0]
    h = x.reshape(B * L, D)
    for layer in range(depth):
        j = layer // 2
        h3 = h.reshape(B, L, D)
        if layer % 2 == 0:
            o = _hgrn_mixer(h3, attn_norm[layer], hgrn_w_in[j], hgrn_lower_bounds, hgrn_gate_norm[j], j)
            w_out = hgrn_w_out[j]
        else:
            o = _dsa_mixer(h3, attn_norm[layer], dsa_w_in[j], rel_bias)
            w_out = dsa_w_out[j]
        h = _ffn(h, o.reshape(B * L, D), w_out, ffn_norm[layer], ffn_w_up[layer], ffn_conv_w[layer],
                 ffn_conv_b[layer], ffn_w_down[layer], final_norm, seq=L, final=(layer == depth - 1))
    return h.reshape(B, L, D)
```
